```python
import math
import jax, jax.numpy as jnp
from jax import lax
import numpy as np

D_MODEL = 1024
BATCH = 2
SEQ = 8192
DEPTH = 1
DEC_BATCH = 16
DEC_SEQ = 64
PAST_LEN = 2048

CHUNK = 64
D_SSM = D_MODEL // 2
SSM_GROUP = 16
N_SSM_GROUPS = D_SSM // SSM_GROUP
SSM_STATE = 64
D_CONV = D_MODEL // 2
CONV_WIDTH = 31
N_BRANCH = 2
D_IN = D_SSM + 2 * D_CONV + N_BRANCH * D_MODEL
N_KEYS = 128
N_EXPERTS = N_KEYS * N_KEYS
PEER_HEADS = 8
PEER_TOPK = 16
D_QUERY = 256
PEER_BLOCK = 128
EPS = 1e-6

kernel_name = 'hybrid_s5_conformer_peer_stream_step'


def _rmsnorm(x, g):
    xf = x.astype(jnp.float32)
    xf = xf * lax.rsqrt(jnp.mean(xf * xf, axis=-1, keepdims=True) + EPS)
    return xf.astype(x.dtype) * g


def _layernorm(x, g, b):
    xf = x.astype(jnp.float32)
    mu = jnp.mean(xf, axis=-1, keepdims=True)
    xc = xf - mu
    xf = xc * lax.rsqrt(jnp.mean(xc * xc, axis=-1, keepdims=True) + EPS)
    return xf.astype(x.dtype) * g + b


def _cplx_affine_combine(e1, e2):
    a1r, a1i, b1r, b1i = e1
    a2r, a2i, b2r, b2i = e2
    return (a2r * a1r - a2i * a1i,
            a2r * a1i + a2i * a1r,
            a2r * b1r - a2i * b1i + b2r,
            a2r * b1i + a2i * b1r + b2i)


def _s5(u, h0_re, h0_im, lam_re, lam_im, log_step, b_re, b_im, c_re, c_im, d_skip):
    n, l, _ = u.shape
    f32 = jnp.float32
    lam_re = lam_re.astype(f32)
    lam_im = lam_im.astype(f32)
    step = jnp.exp(log_step.astype(f32))[:, None]
    decay = jnp.exp(lam_re * step)
    ab_re = decay * jnp.cos(lam_im * step)
    ab_im = decay * jnp.sin(lam_im * step)
    inv = 1.0 / (lam_re * lam_re + lam_im * lam_im)
    k_re = ((ab_re - 1.0) * lam_re + ab_im * lam_im) * inv
    k_im = (ab_im * lam_re - (ab_re - 1.0) * lam_im) * inv
    b_re = b_re.astype(f32)
    b_im = b_im.astype(f32)
    bb_re = k_re[..., None] * b_re - k_im[..., None] * b_im
    bb_im = k_re[..., None] * b_im + k_im[..., None] * b_re
    ug = u.astype(f32).reshape(n, l, N_SSM_GROUPS, SSM_GROUP)
    bu_re = jnp.einsum('nlgh,gph->nlgp', ug, bb_re)
    bu_im = jnp.einsum('nlgh,gph->nlgp', ug, bb_im)
    h0_re = h0_re.astype(f32)
    h0_im = h0_im.astype(f32)
    bu_re = bu_re.at[:, 0].add(ab_re * h0_re - ab_im * h0_im)
    bu_im = bu_im.at[:, 0].add(ab_re * h0_im + ab_im * h0_re)
    a_re = jnp.broadcast_to(ab_re, bu_re.shape)
    a_im = jnp.broadcast_to(ab_im, bu_im.shape)
    _, _, h_re, h_im = lax.associative_scan(_cplx_affine_combine, (a_re, a_im, bu_re, bu_im), axis=1)
    y = (jnp.einsum('nlgp,ghp->nlgh', h_re, c_re.astype(f32))
         - jnp.einsum('nlgp,ghp->nlgh', h_im, c_im.astype(f32))
         + d_skip.astype(f32) * ug)
    return y.reshape(n, l, D_SSM).astype(u.dtype), h_re[:, -1], h_im[:, -1]


def _conv_module(v_in, conv_buf, conv_w, conv_b, ln_g, ln_b, w_conv_out):
    p, q = jnp.split(v_in, 2, axis=-1)
    v = p * jax.nn.sigmoid(q)
    full = jnp.concatenate([conv_buf.astype(v.dtype), v], axis=1)
    out = lax.conv_general_dilated(full, conv_w.astype(v.dtype)[:, None, :], window_strides=(1,),
                                   padding='VALID', dimension_numbers=('NWC', 'WIO', 'NWC'),
                                   feature_group_count=D_CONV) + conv_b
    new_buf = full[:, -(CONV_WIDTH - 1):]
    out = jax.nn.silu(_layernorm(out, ln_g, ln_b))
    return out @ w_conv_out, new_buf


def _peer(xn, w_q, keys1, keys2, u_tab, v_tab):
    n, l, d = xn.shape
    t = n * l
    n_blk = -(-t // PEER_BLOCK)
    flat = jnp.pad(xn.reshape(t, d), ((0, n_blk * PEER_BLOCK - t), (0, 0)))
    half = D_QUERY // 2
    k1 = keys1.astype(jnp.float32)
    k2 = keys2.astype(jnp.float32)

    def block(xb):
        q = (xb @ w_q).astype(jnp.float32).reshape(PEER_BLOCK, PEER_HEADS, 2, half)
        s1 = jnp.einsum('thd,kd->thk', q[:, :, 0], k1)
        s2 = jnp.einsum('thd,kd->thk', q[:, :, 1], k2)
        v1, i1 = lax.top_k(s1, PEER_TOPK)
        v2, i2 = lax.top_k(s2, PEER_TOPK)
        cand = (v1[..., :, None] + v2[..., None, :]).reshape(PEER_BLOCK, PEER_HEADS, PEER_TOPK * PEER_TOPK)
        cidx = (i1[..., :, None] * N_KEYS + i2[..., None, :]).reshape(PEER_BLOCK, PEER_HEADS, PEER_TOPK * PEER_TOPK)
        sc, pos = lax.top_k(cand, PEER_TOPK)
        eidx = jnp.take_along_axis(cidx, pos, axis=-1)
        g = jax.nn.softmax(sc, axis=-1)
        u = jnp.take(u_tab, eidx, axis=0)
        act = jax.nn.gelu(jnp.einsum('thkd,td->thk', u, xb), approximate=False)
        coef = (g * act.astype(jnp.float32)).astype(xb.dtype)
        return jnp.einsum('thk,thkd->td', coef, jnp.take(v_tab, eidx, axis=0))

    out = lax.map(block, flat.reshape(n_blk, PEER_BLOCK, d))
    return out.reshape(n_blk * PEER_BLOCK, d)[:t].reshape(n, l, d)


def _layer(x, h0_re, h0_im, conv_buf, norm_mix, w_in, b_gate, lam_re, lam_im, log_step,
           b_re, b_im, c_re, c_im, d_skip, w_ssm_glu, conv_w, conv_b, ln_g, ln_b, w_conv_out,
           w_out, norm_ffn, w_q, keys1, keys2, u_tab, v_tab):
    xn = _rmsnorm(x, norm_mix)
    z = xn @ w_in
    u_ssm = z[..., :D_SSM]
    v_conv = z[..., D_SSM:D_SSM + 2 * D_CONV]
    gate_pre = z[..., D_SSM + 2 * D_CONV:] + b_gate
    y_ssm, h_re, h_im = _s5(u_ssm, h0_re, h0_im, lam_re, lam_im, log_step, b_re, b_im, c_re, c_im, d_skip)
    pa, qa = jnp.split(jax.nn.gelu(y_ssm, approximate=False) @ w_ssm_glu, 2, axis=-1)
    branch_a = pa * jax.nn.sigmoid(qa)
    branch_b, new_buf = _conv_module(v_conv, conv_buf, conv_w, conv_b, ln_g, ln_b, w_conv_out)
    ga, gb = jnp.split(jax.nn.sigmoid(gate_pre), 2, axis=-1)
    x = x + (ga * branch_a + gb * branch_b) @ w_out
    x = x + _peer(_rmsnorm(x, norm_ffn), w_q, keys1, keys2, u_tab, v_tab)
    return x, h_re, h_im, new_buf


def setup_inputs(seed: int = 0) -> dict:
    key = jax.random.key(seed)
    ks = jax.random.split(key, 32)
    nrm = jax.random.normal
    f32 = jnp.float32
    G, P, H = N_SSM_GROUPS, SSM_STATE, SSM_GROUP
    lam_im0 = math.pi * jnp.arange(P, dtype=f32)
    log_lo, log_hi = math.log(1e-3), math.log(1e-1)
    return {
        'x_prompt': nrm(ks[0], (BATCH, SEQ, D_MODEL), f32),
        'x_sample': nrm(ks[1], (DEC_BATCH, DEC_SEQ, D_MODEL), f32),
        'state_ssm_re': 0.1 * nrm(ks[2], (DEPTH, DEC_BATCH, G, P), f32),
        'state_ssm_im': 0.1 * nrm(ks[3], (DEPTH, DEC_BATCH, G, P), f32),
        'cache_conv': nrm(ks[4], (DEPTH, DEC_BATCH, CONV_WIDTH - 1, D_CONV), f32),
        'norm_mix': 1.0 + 0.02 * nrm(ks[5], (DEPTH, D_MODEL), f32),
        'w_in': nrm(ks[6], (DEPTH, D_MODEL, D_IN), f32) * D_MODEL ** -0.5,
        'b_gate': 0.02 * nrm(ks[7], (DEPTH, N_BRANCH * D_MODEL), f32),
        'ssm_lambda_re': -0.5 + 0.01 * nrm(ks[8], (DEPTH, G, P), f32),
        'ssm_lambda_im': lam_im0 + 0.01 * nrm(ks[9], (DEPTH, G, P), f32),
        'ssm_log_step': jax.random.uniform(ks[10], (DEPTH, G), f32, log_lo, log_hi),
        'ssm_b_re': nrm(ks[11], (DEPTH, G, P, H), f32) * (2.0 * H) ** -0.5,
        'ssm_b_im': nrm(ks[12], (DEPTH, G, P, H), f32) * (2.0 * H) ** -0.5,
        'ssm_c_re': nrm(ks[13], (DEPTH, G, H, P), f32) * P ** -0.5,
        'ssm_c_im': nrm(ks[14], (DEPTH, G, H, P), f32) * P ** -0.5,
        'ssm_d': nrm(ks[15], (DEPTH, G, H), f32),
        'w_ssm_glu': nrm(ks[16], (DEPTH, D_SSM, 2 * D_MODEL), f32) * D_SSM ** -0.5,
        'conv_w': nrm(ks[17], (DEPTH, CONV_WIDTH, D_CONV), f32) * CONV_WIDTH ** -0.5,
        'conv_b': 0.02 * nrm(ks[18], (DEPTH, D_CONV), f32),
        'conv_ln_g': 1.0 + 0.02 * nrm(ks[19], (DEPTH, D_CONV), f32),
        'conv_ln_b': 0.02 * nrm(ks[20], (DEPTH, D_CONV), f32),
        'w_conv_out': nrm(ks[21], (DEPTH, D_CONV, D_MODEL), f32) * D_CONV ** -0.5,
        'w_out': nrm(ks[22], (DEPTH, D_MODEL, D_MODEL), f32) * D_MODEL ** -0.5,
        'norm_ffn': 1.0 + 0.02 * nrm(ks[23], (DEPTH, D_MODEL), f32),
        'peer_w_q': nrm(ks[24], (DEPTH, D_MODEL, PEER_HEADS * D_QUERY), f32) * D_MODEL ** -0.5,
        'peer_keys1': nrm(ks[25], (DEPTH, N_KEYS, D_QUERY // 2), f32) * (D_QUERY // 2) ** -0.5,
        'peer_keys2': nrm(ks[26], (DEPTH, N_KEYS, D_QUERY // 2), f32) * (D_QUERY // 2) ** -0.5,
        'peer_u': nrm(ks[27], (DEPTH, N_EXPERTS, D_MODEL), f32) * D_MODEL ** -0.5,
        'peer_v': nrm(ks[28], (DEPTH, N_EXPERTS, D_MODEL), f32) * PEER_HEADS ** -0.5,
        'norm_final': 1.0 + 0.02 * nrm(ks[29], (D_MODEL,), f32),
    }


def reference(x_prompt, x_sample, state_ssm_re, state_ssm_im, cache_conv, norm_mix, w_in, b_gate,
              ssm_lambda_re, ssm_lambda_im, ssm_log_step, ssm_b_re, ssm_b_im, ssm_c_re, ssm_c_im,
              ssm_d, w_ssm_glu, conv_w, conv_b, conv_ln_g, conv_ln_b, w_conv_out, w_out, norm_ffn,
              peer_w_q, peer_keys1, peer_keys2, peer_u, peer_v, norm_final):
    n_p = x_prompt.shape[0]
    zero_ssm = jnp.zeros((n_p, N_SSM_GROUPS, SSM_STATE), jnp.float32)
    zero_conv = jnp.zeros((n_p, CONV_WIDTH - 1, D_CONV), x_prompt.dtype)
    hp, hs = x_prompt, x_sample
    re_p, im_p, cv_p, re_s, im_s, cv_s = [], [], [], [], [], []
    for l in range(DEPTH):
        w = (norm_mix[l], w_in[l], b_gate[l], ssm_lambda_re[l], ssm_lambda_im[l], ssm_log_step[l],
             ssm_b_re[l], ssm_b_im[l], ssm_c_re[l], ssm_c_im[l], ssm_d[l], w_ssm_glu[l], conv_w[l],
             conv_b[l], conv_ln_g[l], conv_ln_b[l], w_conv_out[l], w_out[l], norm_ffn[l],
             peer_w_q[l], peer_keys1[l], peer_keys2[l], peer_u[l], peer_v[l])
        hp, r_p, i_p, c_p = _layer(hp, zero_ssm, zero_ssm, zero_conv, *w)
        hs, r_s, i_s, c_s = _layer(hs, state_ssm_re[l], state_ssm_im[l], cache_conv[l], *w)
        re_p.append(r_p); im_p.append(i_p); cv_p.append(c_p)
        re_s.append(r_s); im_s.append(i_s); cv_s.append(c_s)
    y_prompt = _rmsnorm(hp, norm_final)
    y_sample = _rmsnorm(hs, norm_final)
    return (y_prompt, y_sample, jnp.stack(re_p), jnp.stack(im_p), jnp.stack(cv_p),
            jnp.stack(re_s), jnp.stack(im_s), jnp.stack(cv_s))
```

```python
import functools
import math

import jax
import jax.numpy as jnp
from jax import lax
from jax.experimental import pallas as pl
from jax.experimental.pallas import tpu as pltpu

F32 = jnp.float32
BF16 = jnp.bfloat16

EPS = 1e-6
CHUNK = 64
SSM_GROUP = 16
SSM_STATE = 64
CONV_WIDTH = 31
HIST_ROWS = 32
N_KEYS = 128
PEER_HEADS = 8
PEER_TOPK = 16
LANES = 128
SUBLANES = 8
TOKEN_TILE = 512
EXPERT_BLOCK = 1024
VMEM_LIMIT = 56 * 1024 * 1024
NEG_INF = float("-inf")


def _rms(x, g):
    ms = jnp.mean(x * x, axis=-1, keepdims=True)
    return (x * lax.rsqrt(ms + EPS)) * g


def _gelu(x):
    return 0.5 * x * (1.0 + lax.erf(x * (1.0 / math.sqrt(2.0))))


def _sigmoid(x):
    return 1.0 / (1.0 + jnp.exp(-x))


def _const_spec(shape):
    nd = len(shape)
    return pl.BlockSpec(shape, lambda *_: (0,) * nd)


def _in_proj_kernel(x_ref, g_ref, w_ref, bg_ref, u_ref, v_ref, gate_ref, *, d_ssm, d_conv):
    xn = _rms(x_ref[...], g_ref[...]).astype(BF16)
    u_ref[...] = jnp.dot(xn, w_ref[:, :d_ssm], preferred_element_type=F32)
    pq = jnp.dot(xn, w_ref[:, d_ssm:d_ssm + 2 * d_conv], preferred_element_type=F32)
    v_ref[...] = pq[:, :d_conv] * _sigmoid(pq[:, d_conv:])
    gl = jnp.dot(xn, w_ref[:, d_ssm + 2 * d_conv:], preferred_element_type=F32)
    gate_ref[...] = _sigmoid(gl + bg_ref[...]).astype(BF16)


def _in_proj(x, g, w_in, b_gate, d_ssm, d_conv):
    t, d = x.shape
    d_in = w_in.shape[1]
    n_gate = d_in - d_ssm - 2 * d_conv
    tm = TOKEN_TILE
    return pl.pallas_call(
        functools.partial(_in_proj_kernel, d_ssm=d_ssm, d_conv=d_conv),
        grid=(t // tm,),
        in_specs=[pl.BlockSpec((tm, d), lambda i: (i, 0)),
                  _const_spec((1, d)), _const_spec((d, d_in)), _const_spec((1, n_gate))],
        out_specs=[pl.BlockSpec((tm, d_ssm), lambda i: (i, 0)),
                   pl.BlockSpec((tm, d_conv), lambda i: (i, 0)),
                   pl.BlockSpec((tm, n_gate), lambda i: (i, 0))],
        out_shape=[jax.ShapeDtypeStruct((t, d_ssm), F32),
                   jax.ShapeDtypeStruct((t, d_conv), F32),
                   jax.ShapeDtypeStruct((t, n_gate), BF16)],
        compiler_params=pltpu.CompilerParams(dimension_semantics=("arbitrary",),
                                             vmem_limit_bytes=VMEM_LIMIT),
        name="in_proj",
    )(x, g, w_in, b_gate)


def _ssm_kernel(flags_ref, u_ref, h0_ref, wb_ref, wc_ref, d_ref, wglu_ref, lvl_ref, pw_ref,
                a_ref, hend_ref, bu_ref, cre_ref, cim_ref, *, n_state, d_model):
    tile = u_ref.shape[0]
    chunks = tile // CHUNK
    step = pl.program_id(0)

    @pl.when(step == 0)
    def _():
        cre_ref[...] = jnp.zeros_like(cre_ref)
        cim_ref[...] = jnp.zeros_like(cim_ref)

    u = u_ref[...]
    bu_ref[...] = jnp.dot(u.astype(BF16), wb_ref[...], preferred_element_type=F32)

    for j in range(chunks):
        start = flags_ref[step * chunks + j] != 0
        c_re = jnp.where(start, h0_ref[j:j + 1, :n_state], cre_ref[...])
        c_im = jnp.where(start, h0_ref[j:j + 1, n_state:], cim_ref[...])

        def block(b, carry, j=j):
            c_re, c_im = carry
            r0 = pl.multiple_of((j * (CHUNK // SUBLANES) + b) * SUBLANES, SUBLANES)
            x_re = bu_ref[pl.ds(r0, SUBLANES), :n_state]
            x_im = bu_ref[pl.ds(r0, SUBLANES), n_state:]
            for lvl, k in enumerate((1, 2, 4)):
                m_re = lvl_ref[lvl, 0]
                m_im = lvl_ref[lvl, 1]
                t_re = pltpu.roll(x_re, k, 0)
                t_im = pltpu.roll(x_im, k, 0)
                x_re, x_im = (x_re + m_re * t_re - m_im * t_im,
                              x_im + m_re * t_im + m_im * t_re)
            p_re = pw_ref[0]
            p_im = pw_ref[1]
            h_re = x_re + p_re * c_re - p_im * c_im
            h_im = x_im + p_re * c_im + p_im * c_re
            bu_ref[pl.ds(r0, SUBLANES), :n_state] = h_re
            bu_ref[pl.ds(r0, SUBLANES), n_state:] = h_im
            return h_re[SUBLANES - 1:], h_im[SUBLANES - 1:]

        c_re, c_im = lax.fori_loop(0, CHUNK // SUBLANES, block, (c_re, c_im))
        cre_ref[...] = c_re
        cim_ref[...] = c_im
        hend_ref[j:j + 1, :n_state] = c_re
        hend_ref[j:j + 1, n_state:] = c_im

    y = jnp.dot(bu_ref[...].astype(BF16), wc_ref[...], preferred_element_type=F32) + d_ref[...] * u
    pq = jnp.dot(_gelu(y).astype(BF16), wglu_ref[...], preferred_element_type=F32)
    a_ref[...] = (pq[:, :d_model] * _sigmoid(pq[:, d_model:])).astype(BF16)


def _ssm(flags, u, h0, wb, wc, d_skip, wglu, lvl, pw, d_model):
    t, d_ssm = u.shape
    n_state = wb.shape[1] // 2
    tm = TOKEN_TILE
    chunks = tm // CHUNK
    grid_spec = pltpu.PrefetchScalarGridSpec(
        num_scalar_prefetch=1,
        grid=(t // tm,),
        in_specs=[pl.BlockSpec((tm, d_ssm), lambda i, f: (i, 0)),
                  pl.BlockSpec((chunks, 2 * n_state), lambda i, f: (i, 0)),
                  _const_spec(wb.shape), _const_spec(wc.shape), _const_spec(d_skip.shape),
                  _const_spec(wglu.shape), _const_spec(lvl.shape), _const_spec(pw.shape)],
        out_specs=[pl.BlockSpec((tm, d_model), lambda i, f: (i, 0)),
                   pl.BlockSpec((chunks, 2 * n_state), lambda i, f: (i, 0))],
        scratch_shapes=[pltpu.VMEM((tm, 2 * n_state), F32),
                        pltpu.VMEM((1, n_state), F32), pltpu.VMEM((1, n_state), F32)],
    )
    return pl.pallas_call(
        functools.partial(_ssm_kernel, n_state=n_state, d_model=d_model),
        grid_spec=grid_spec,
        out_shape=[jax.ShapeDtypeStruct((t, d_model), BF16),
                   jax.ShapeDtypeStruct((t // CHUNK, 2 * n_state), F32)],
        compiler_params=pltpu.CompilerParams(dimension_semantics=("arbitrary",),
                                             vmem_limit_bytes=VMEM_LIMIT),
        name="ssm",
    )(flags, u, h0, wb, wc, d_skip, wglu, lvl, pw)


def _mix_kernel(flags_ref, v_ref, hist_ref, cw_ref, cb_ref, lg_ref, lb_ref, wco_ref,
                gate_ref, a_ref, x_ref, wout_ref, x1_ref, win_ref, act_ref, *, d_model):
    tile = v_ref.shape[0]
    chunks = tile // CHUNK
    step = pl.program_id(0)
    lead = HIST_ROWS - (CONV_WIDTH - 1)

    @pl.when(step == 0)
    def _():
        win_ref[...] = jnp.zeros_like(win_ref)

    def chunk_body(j, carry):
        r0 = pl.multiple_of(j * CHUNK, CHUNK)
        start = flags_ref[step * chunks + j] != 0
        win_ref[:HIST_ROWS] = jnp.where(start, hist_ref[j], win_ref[CHUNK:])
        win_ref[HIST_ROWS:] = v_ref[pl.ds(r0, CHUNK), :]
        acc = cw_ref[0:1, :] * win_ref[lead:lead + CHUNK, :]
        for k in range(1, CONV_WIDTH):
            acc = acc + cw_ref[k:k + 1, :] * win_ref[lead + k:lead + k + CHUNK, :]
        acc = acc + cb_ref[...]
        mu = jnp.mean(acc, axis=-1, keepdims=True)
        xc = acc - mu
        ln = (xc * lax.rsqrt(jnp.mean(xc * xc, axis=-1, keepdims=True) + EPS)) * lg_ref[...] + lb_ref[...]
        act_ref[pl.ds(r0, CHUNK), :] = (ln * _sigmoid(ln)).astype(BF16)
        return carry

    lax.fori_loop(0, chunks, chunk_body, 0)
    branch_b = jnp.dot(act_ref[...], wco_ref[...], preferred_element_type=F32)
    ga = gate_ref[:, :d_model].astype(F32)
    gb = gate_ref[:, d_model:].astype(F32)
    m = ga * a_ref[...].astype(F32) + gb * branch_b
    x1_ref[...] = x_ref[...] + jnp.dot(m.astype(BF16), wout_ref[...], preferred_element_type=F32)


def _mix(flags, v, hist, conv_w, conv_b, ln_g, ln_b, w_conv_out, gate, branch_a, x, w_out):
    t, d_conv = v.shape
    d_model = x.shape[1]
    tm = TOKEN_TILE
    chunks = tm // CHUNK
    grid_spec = pltpu.PrefetchScalarGridSpec(
        num_scalar_prefetch=1,
        grid=(t // tm,),
        in_specs=[pl.BlockSpec((tm, d_conv), lambda i, f: (i, 0)),
                  pl.BlockSpec((chunks, HIST_ROWS, d_conv), lambda i, f: (i, 0, 0)),
                  _const_spec(conv_w.shape), _const_spec(conv_b.shape), _const_spec(ln_g.shape),
                  _const_spec(ln_b.shape), _const_spec(w_conv_out.shape),
                  pl.BlockSpec((tm, 2 * d_model), lambda i, f: (i, 0)),
                  pl.BlockSpec((tm, d_model), lambda i, f: (i, 0)),
                  pl.BlockSpec((tm, d_model), lambda i, f: (i, 0)),
                  _const_spec(w_out.shape)],
        out_specs=pl.BlockSpec((tm, d_model), lambda i, f: (i, 0)),
        scratch_shapes=[pltpu.VMEM((HIST_ROWS + CHUNK, d_conv), F32),
                        pltpu.VMEM((tm, d_conv), BF16)],
    )
    return pl.pallas_call(
        functools.partial(_mix_kernel, d_model=d_model),
        grid_spec=grid_spec,
        out_shape=jax.ShapeDtypeStruct((t, d_model), F32),
        compiler_params=pltpu.CompilerParams(dimension_semantics=("arbitrary",),
                                             vmem_limit_bytes=VMEM_LIMIT),
        name="mix",
    )(flags, v, hist, conv_w, conv_b, ln_g, ln_b, w_conv_out, gate, branch_a, x, w_out)


def _top16(s, iota):
    iota16 = iota[:PEER_TOPK]

    def rnd(r, carry):
        work, rank, vals = carry
        rf = r.astype(F32)
        m = jnp.max(work, axis=0, keepdims=True)
        first = jnp.min(jnp.where(work == m, iota, float(N_KEYS)), axis=0, keepdims=True)
        sel = iota == first
        rank = jnp.where(sel, rf, rank)
        work = jnp.where(sel, NEG_INF, work)
        vals = jnp.where(iota16 == rf, m, vals)
        return work, rank, vals

    init = (s, jnp.full(s.shape, float(PEER_TOPK), F32), jnp.zeros((PEER_TOPK, s.shape[1]), F32))
    _, rank, vals = lax.fori_loop(0, PEER_TOPK, rnd, init)
    return vals, rank


def _merge16(v1, v2, iota16):
    m0 = v1[0:1] + v2[0:1]

    def rnd(_, carry):
        n, front, z = carry
        m = jnp.max(front, axis=0, keepdims=True)
        a_sel = jnp.min(jnp.where(front == m, iota16, float(PEER_TOPK)), axis=0, keepdims=True)
        sel = iota16 == a_sel
        n = jnp.where(sel, n + 1.0, n)
        k = jnp.max(jnp.where(sel, n, 0.0), axis=0, keepdims=True)
        v2k = jnp.max(jnp.where(iota16 == k, v2, NEG_INF), axis=0, keepdims=True)
        front = jnp.where(sel, v1 + v2k, front)
        return n, front, z + jnp.exp(m - m0)

    init = (jnp.zeros_like(v1), v1 + v2[0:1], jnp.zeros_like(m0))
    n, _, z = lax.fori_loop(0, PEER_TOPK, rnd, init)
    return n, z


def _query_kernel(x1_ref, g_ref, wq_ref, k1_ref, k2_ref,
                  xn_ref, key2_ref, thr_ref, p1_ref, p2_ref, s1_ref, s2_ref):
    tile = x1_ref.shape[0]
    lane_tiles = tile // LANES
    half = k1_ref.shape[1]
    xn = _rms(x1_ref[...], g_ref[...]).astype(BF16)
    xn_ref[...] = xn
    q = jnp.dot(xn, wq_ref[...], preferred_element_type=F32).astype(BF16)
    nt = (((1,), (1,)), ((), ()))
    for h in range(PEER_HEADS):
        q1 = q[:, 2 * h * half:(2 * h + 1) * half]
        q2 = q[:, (2 * h + 1) * half:(2 * h + 2) * half]
        s1 = lax.dot_general(k1_ref[...], q1, nt, preferred_element_type=F32)
        s2 = lax.dot_general(k2_ref[...], q2, nt, preferred_element_type=F32)
        for c in range(lane_tiles):
            s1_ref[h, c] = s1[:, c * LANES:(c + 1) * LANES]
            s2_ref[h, c] = s2[:, c * LANES:(c + 1) * LANES]

    def route(idx, carry):
        h = idx // lane_tiles
        c = idx % lane_tiles
        iota = lax.broadcasted_iota(jnp.int32, (N_KEYS, LANES), 0).astype(F32)
        s1 = s1_ref[h, c]
        s2 = s2_ref[h, c]
        v1, rank1 = _top16(s1, iota)
        v2, rank2 = _top16(s2, iota)
        n, z = _merge16(v1, v2, iota[:PEER_TOPK])
        cnt = jnp.zeros_like(rank1)
        for a in range(PEER_TOPK):
            cnt = jnp.where(rank1 == float(a), n[a:a + 1], cnt)
        key2_ref[h, c] = -rank2
        thr_ref[h, c] = 1.0 - cnt
        p1_ref[h, c] = jnp.exp(s1 - v1[0:1]) / z
        p2_ref[h, c] = jnp.exp(s2 - v2[0:1])
        return carry

    lax.fori_loop(0, PEER_HEADS * lane_tiles, route, 0)


def _query(x1, g, w_q, keys1, keys2):
    t, d = x1.shape
    tm = TOKEN_TILE
    lane_tiles = tm // LANES
    tab_block = (PEER_HEADS, lane_tiles, N_KEYS, LANES)
    tab_spec = pl.BlockSpec(tab_block, lambda i: (0, i, 0, 0))
    tab_shape = jax.ShapeDtypeStruct((PEER_HEADS, t // LANES, N_KEYS, LANES), F32)
    return pl.pallas_call(
        _query_kernel,
        grid=(t // tm,),
        in_specs=[pl.BlockSpec((tm, d), lambda i: (i, 0)), _const_spec(g.shape),
                  _const_spec(w_q.shape), _const_spec(keys1.shape), _const_spec(keys2.shape)],
        out_specs=[pl.BlockSpec((tm, d), lambda i: (i, 0)), tab_spec, tab_spec, tab_spec, tab_spec],
        out_shape=[jax.ShapeDtypeStruct((t, d), BF16), tab_shape, tab_shape, tab_shape, tab_shape],
        scratch_shapes=[pltpu.VMEM(tab_block, F32), pltpu.VMEM(tab_block, F32)],
        compiler_params=pltpu.CompilerParams(dimension_semantics=("arbitrary",),
                                             vmem_limit_bytes=VMEM_LIMIT),
        name="query",
    )(x1, g, w_q, keys1, keys2)


def _peer_kernel(xn_ref, u_ref, vt_ref, key2_ref, thr_ref, p1_ref, p2_ref, x1_ref, g_ref,
                 y_ref, acc_ref, act_ref, coef_ref):
    kb = pl.program_id(1)
    lane_tiles = xn_ref.shape[0] // LANES
    rows = u_ref.shape[0] // N_KEYS

    @pl.when(kb == 0)
    def _():
        acc_ref[...] = jnp.zeros_like(acc_ref)

    nt = (((1,), (1,)), ((), ()))
    act_ref[...] = lax.dot_general(u_ref[...], xn_ref[...], nt, preferred_element_type=F32)

    for il in range(rows):
        i = kb * rows + il
        for c in range(lane_tiles):
            w = jnp.zeros((N_KEYS, LANES), F32)
            for h in range(PEER_HEADS):
                thr = thr_ref[h, c, pl.ds(i, 1), :]
                p1 = p1_ref[h, c, pl.ds(i, 1), :]
                w = w + jnp.where(key2_ref[h, c] >= thr, p2_ref[h, c], 0.0) * p1
            blk = (slice(il * N_KEYS, (il + 1) * N_KEYS), slice(c * LANES, (c + 1) * LANES))
            coef_ref[blk] = (w * _gelu(act_ref[blk])).astype(BF16)

    acc_ref[...] += jnp.dot(vt_ref[...], coef_ref[...], preferred_element_type=F32)

    @pl.when(kb == pl.num_programs(1) - 1)
    def _():
        y_ref[...] = _rms(x1_ref[...] + acc_ref[...].T, g_ref[...])


def _peer(xn, u_tab, vt_tab, key2, thr, p1, p2, x1, g):
    t, d = xn.shape
    n_exp = u_tab.shape[0]
    tt, eb = TOKEN_TILE, EXPERT_BLOCK
    lane_tiles = tt // LANES
    tab_spec = pl.BlockSpec((PEER_HEADS, lane_tiles, N_KEYS, LANES), lambda i, k: (0, i, 0, 0))
    return pl.pallas_call(
        _peer_kernel,
        grid=(t // tt, n_exp // eb),
        in_specs=[pl.BlockSpec((tt, d), lambda i, k: (i, 0)),
                  pl.BlockSpec((eb, d), lambda i, k: (k, 0)),
                  pl.BlockSpec((d, eb), lambda i, k: (0, k)),
                  tab_spec, tab_spec, tab_spec, tab_spec,
                  pl.BlockSpec((tt, d), lambda i, k: (i, 0)),
                  pl.BlockSpec((1, d), lambda i, k: (0, 0))],
        out_specs=pl.BlockSpec((tt, d), lambda i, k: (i, 0)),
        out_shape=jax.ShapeDtypeStruct((t, d), F32),
        scratch_shapes=[pltpu.VMEM((d, tt), F32), pltpu.VMEM((eb, tt), F32), pltpu.VMEM((eb, tt), BF16)],
        compiler_params=pltpu.CompilerParams(dimension_semantics=("arbitrary", "arbitrary"),
                                             vmem_limit_bytes=VMEM_LIMIT),
        name="peer",
    )(xn, u_tab, vt_tab, key2, thr, p1, p2, x1, g)


def _ssm_params(lam_re, lam_im, log_step, b_re, b_im, c_re, c_im, d_skip):
    g, p = lam_re.shape
    hh = b_re.shape[2]
    step = jnp.exp(log_step)[:, None]
    decay = jnp.exp(lam_re * step)
    ab_re = decay * jnp.cos(lam_im * step)
    ab_im = decay * jnp.sin(lam_im * step)
    inv = 1.0 / (lam_re * lam_re + lam_im * lam_im)
    k_re = ((ab_re - 1.0) * lam_re + ab_im * lam_im) * inv
    k_im = (ab_im * lam_re - (ab_re - 1.0) * lam_im) * inv
    bb_re = k_re[..., None] * b_re - k_im[..., None] * b_im
    bb_im = k_re[..., None] * b_im + k_im[..., None] * b_re
    eye = jnp.eye(g, dtype=F32)

    def in_mat(bb):
        return jnp.einsum("gph,gk->ghkp", bb, eye).reshape(g * hh, g * p)

    def out_mat(cc):
        return jnp.einsum("ghp,gk->gpkh", cc, eye).reshape(g * p, g * hh)

    wb = jnp.concatenate([in_mat(bb_re), in_mat(bb_im)], axis=1).astype(BF16)
    wc = jnp.concatenate([out_mat(c_re), -out_mat(c_im)], axis=0).astype(BF16)

    a_re = ab_re.reshape(1, g * p)
    a_im = ab_im.reshape(1, g * p)
    pows = [(a_re, a_im)]
    for _ in range(SUBLANES - 1):
        q_re, q_im = pows[-1]
        pows.append((q_re * a_re - q_im * a_im, q_re * a_im + q_im * a_re))
    row = jnp.arange(SUBLANES)[:, None]
    lvl = jnp.stack([jnp.stack([jnp.where(row >= k, pows[k - 1][0], 0.0),
                                jnp.where(row >= k, pows[k - 1][1], 0.0)]) for k in (1, 2, 4)])
    pw = jnp.stack([jnp.concatenate([q[0] for q in pows], axis=0),
                    jnp.concatenate([q[1] for q in pows], axis=0)])
    return wb, wc, d_skip.reshape(1, g * hh), lvl, pw


def kernel(x_prompt, x_sample, state_ssm_re, state_ssm_im, cache_conv, norm_mix, w_in, b_gate,
           ssm_lambda_re, ssm_lambda_im, ssm_log_step, ssm_b_re, ssm_b_im, ssm_c_re, ssm_c_im,
           ssm_d, w_ssm_glu, conv_w, conv_b, conv_ln_g, conv_ln_b, w_conv_out, w_out, norm_ffn,
           peer_w_q, peer_keys1, peer_keys2, peer_u, peer_v, norm_final):
    depth = w_in.shape[0]
    assert depth == 1, "single-layer stack only"
    n_p, l_p, d = x_prompt.shape
    n_s, l_s, _ = x_sample.shape
    assert l_p % CHUNK == 0 and l_s % CHUNK == 0
    t_p, t_s = n_p * l_p, n_s * l_s
    t = t_p + t_s
    assert t % TOKEN_TILE == 0
    g, p = ssm_lambda_re.shape[1:]
    n_state = g * p
    d_ssm = g * ssm_b_re.shape[3]
    d_conv = conv_w.shape[2]
    hist_len = CONV_WIDTH - 1

    x = jnp.concatenate([x_prompt.reshape(t_p, d), x_sample.reshape(t_s, d)], axis=0)

    cp, cs = l_p // CHUNK, l_s // CHUNK
    flag_p = jnp.zeros((n_p, cp), jnp.int32).at[:, 0].set(1).reshape(-1)
    flag_s = jnp.zeros((n_s, cs), jnp.int32).at[:, 0].set(1).reshape(-1)
    flags = jnp.concatenate([flag_p, flag_s])
    h0_s = jnp.concatenate([state_ssm_re[0].reshape(n_s, n_state), state_ssm_im[0].reshape(n_s, n_state)], axis=1)
    h0_s = jnp.zeros((n_s, cs, 2 * n_state), F32).at[:, 0].set(h0_s).reshape(n_s * cs, 2 * n_state)
    h0 = jnp.concatenate([jnp.zeros((n_p * cp, 2 * n_state), F32), h0_s], axis=0)
    hist_s = jnp.pad(cache_conv[0], ((0, 0), (HIST_ROWS - hist_len, 0), (0, 0)))
    hist_s = jnp.zeros((n_s, cs, HIST_ROWS, d_conv), F32).at[:, 0].set(hist_s).reshape(n_s * cs, HIST_ROWS, d_conv)
    hist = jnp.concatenate([jnp.zeros((n_p * cp, HIST_ROWS, d_conv), F32), hist_s], axis=0)

    wb, wc, d_skip, lvl, pw = _ssm_params(ssm_lambda_re[0], ssm_lambda_im[0], ssm_log_step[0],
                                          ssm_b_re[0], ssm_b_im[0], ssm_c_re[0], ssm_c_im[0], ssm_d[0])

    u, v, gate = _in_proj(x, norm_mix, w_in[0].astype(BF16), b_gate, d_ssm, d_conv)
    branch_a, hend = _ssm(flags, u, h0, wb, wc, d_skip, w_ssm_glu[0].astype(BF16), lvl, pw, d)
    cw = jnp.pad(conv_w[0], ((0, HIST_ROWS - CONV_WIDTH), (0, 0)))
    x1 = _mix(flags, v, hist, cw, conv_b, conv_ln_g, conv_ln_b, w_conv_out[0].astype(BF16),
              gate, branch_a, x, w_out[0].astype(BF16))
    xn2, key2, thr, p1, p2 = _query(x1, norm_ffn, peer_w_q[0].astype(BF16),
                                    peer_keys1[0].astype(BF16), peer_keys2[0].astype(BF16))
    y = _peer(xn2, peer_u[0].astype(BF16), peer_v[0].T.astype(BF16), key2, thr, p1, p2, x1,
              norm_final.reshape(1, d))

    y_prompt = y[:t_p].reshape(n_p, l_p, d)
    y_sample = y[t_p:].reshape(n_s, l_s, d)
    last_p = hend[:n_p * cp].reshape(n_p, cp, 2 * n_state)[:, -1]
    last_s = hend[n_p * cp:].reshape(n_s, cs, 2 * n_state)[:, -1]
    v_p = v[:t_p].reshape(n_p, l_p, d_conv)[:, l_p - hist_len:]
    v_s = v[t_p:].reshape(n_s, l_s, d_conv)[:, l_s - hist_len:]
    return (y_prompt, y_sample,
            last_p[:, :n_state].reshape(1, n_p, g, p), last_p[:, n_state:].reshape(1, n_p, g, p), v_p[None],
            last_s[:, :n_state].reshape(1, n_s, g, p), last_s[:, n_state:].reshape(1, n_s, g, p), v_s[None])
```

```python
import functools
import math

import jax
import jax.numpy as jnp
from jax import lax
from jax.experimental import pallas as pl
from jax.experimental.pallas import tpu as pltpu

F32 = jnp.float32
BF16 = jnp.bfloat16

EPS = 1e-6
CHUNK = 64
SSM_GROUP = 16
SSM_STATE = 64
CONV_WIDTH = 31
HIST_ROWS = 32
N_KEYS = 128
PEER_HEADS = 8
PEER_TOPK = 16
LANES = 128
SUBLANES = 8
TOKEN_TILE = 512
EXPERT_BLOCK = 1024
JROWS = 16
VMEM_LIMIT = 56 * 1024 * 1024
NEG_INF = float("-inf")


def _rms(x, g):
    ms = jnp.mean(x * x, axis=-1, keepdims=True)
    return (x * lax.rsqrt(ms + EPS)) * g


def _gelu(x):
    return 0.5 * x * (1.0 + lax.erf(x * (1.0 / math.sqrt(2.0))))


def _sigmoid(x):
    return 1.0 / (1.0 + jnp.exp(-x))


def _const_spec(shape):
    nd = len(shape)
    return pl.BlockSpec(shape, lambda *_: (0,) * nd)


def _in_proj_kernel(x_ref, g_ref, w_ref, bg_ref, u_ref, v_ref, gate_ref, *, d_ssm, d_conv):
    xn = _rms(x_ref[...], g_ref[...]).astype(BF16)
    u_ref[...] = jnp.dot(xn, w_ref[:, :d_ssm], preferred_element_type=F32)
    pq = jnp.dot(xn, w_ref[:, d_ssm:d_ssm + 2 * d_conv], preferred_element_type=F32)
    v_ref[...] = pq[:, :d_conv] * _sigmoid(pq[:, d_conv:])
    gl = jnp.dot(xn, w_ref[:, d_ssm + 2 * d_conv:], preferred_element_type=F32)
    gate_ref[...] = _sigmoid(gl + bg_ref[...]).astype(BF16)


def _in_proj(x, g, w_in, b_gate, d_ssm, d_conv):
    t, d = x.shape
    d_in = w_in.shape[1]
    n_gate = d_in - d_ssm - 2 * d_conv
    tm = TOKEN_TILE
    return pl.pallas_call(
        functools.partial(_in_proj_kernel, d_ssm=d_ssm, d_conv=d_conv),
        grid=(t // tm,),
        in_specs=[pl.BlockSpec((tm, d), lambda i: (i, 0)),
                  _const_spec((1, d)), _const_spec((d, d_in)), _const_spec((1, n_gate))],
        out_specs=[pl.BlockSpec((tm, d_ssm), lambda i: (i, 0)),
                   pl.BlockSpec((tm, d_conv), lambda i: (i, 0)),
                   pl.BlockSpec((tm, n_gate), lambda i: (i, 0))],
        out_shape=[jax.ShapeDtypeStruct((t, d_ssm), F32),
                   jax.ShapeDtypeStruct((t, d_conv), F32),
                   jax.ShapeDtypeStruct((t, n_gate), BF16)],
        compiler_params=pltpu.CompilerParams(dimension_semantics=("arbitrary",),
                                             vmem_limit_bytes=VMEM_LIMIT),
        name="in_proj",
    )(x, g, w_in, b_gate)


def _ssm_kernel(flags_ref, u_ref, h0_ref, wb_ref, wc_ref, d_ref, wglu_ref, lvl_ref, pw_ref,
                a_ref, hend_ref, bu_ref, cre_ref, cim_ref, *, n_state, d_model):
    tile = u_ref.shape[0]
    chunks = tile // CHUNK
    step = pl.program_id(0)

    @pl.when(step == 0)
    def _():
        cre_ref[...] = jnp.zeros_like(cre_ref)
        cim_ref[...] = jnp.zeros_like(cim_ref)

    u = u_ref[...]
    bu_ref[...] = jnp.dot(u.astype(BF16), wb_ref[...], preferred_element_type=F32)

    for j in range(chunks):
        start = flags_ref[step * chunks + j] != 0
        c_re = jnp.where(start, h0_ref[j:j + 1, :n_state], cre_ref[...])
        c_im = jnp.where(start, h0_ref[j:j + 1, n_state:], cim_ref[...])

        def block(b, carry, j=j):
            c_re, c_im = carry
            r0 = pl.multiple_of((j * (CHUNK // SUBLANES) + b) * SUBLANES, SUBLANES)
            x_re = bu_ref[pl.ds(r0, SUBLANES), :n_state]
            x_im = bu_ref[pl.ds(r0, SUBLANES), n_state:]
            for lvl, k in enumerate((1, 2, 4)):
                m_re = lvl_ref[lvl, 0]
                m_im = lvl_ref[lvl, 1]
                t_re = pltpu.roll(x_re, k, 0)
                t_im = pltpu.roll(x_im, k, 0)
                x_re, x_im = (x_re + m_re * t_re - m_im * t_im,
                              x_im + m_re * t_im + m_im * t_re)
            p_re = pw_ref[0]
            p_im = pw_ref[1]
            h_re = x_re + p_re * c_re - p_im * c_im
            h_im = x_im + p_re * c_im + p_im * c_re
            bu_ref[pl.ds(r0, SUBLANES), :n_state] = h_re
            bu_ref[pl.ds(r0, SUBLANES), n_state:] = h_im
            return h_re[SUBLANES - 1:], h_im[SUBLANES - 1:]

        c_re, c_im = lax.fori_loop(0, CHUNK // SUBLANES, block, (c_re, c_im))
        cre_ref[...] = c_re
        cim_ref[...] = c_im
        hend_ref[j:j + 1, :n_state] = c_re
        hend_ref[j:j + 1, n_state:] = c_im

    y = jnp.dot(bu_ref[...].astype(BF16), wc_ref[...], preferred_element_type=F32) + d_ref[...] * u
    pq = jnp.dot(_gelu(y).astype(BF16), wglu_ref[...], preferred_element_type=F32)
    a_ref[...] = (pq[:, :d_model] * _sigmoid(pq[:, d_model:])).astype(BF16)


def _ssm(flags, u, h0, wb, wc, d_skip, wglu, lvl, pw, d_model):
    t, d_ssm = u.shape
    n_state = wb.shape[1] // 2
    tm = TOKEN_TILE
    chunks = tm // CHUNK
    grid_spec = pltpu.PrefetchScalarGridSpec(
        num_scalar_prefetch=1,
        grid=(t // tm,),
        in_specs=[pl.BlockSpec((tm, d_ssm), lambda i, f: (i, 0)),
                  pl.BlockSpec((chunks, 2 * n_state), lambda i, f: (i, 0)),
                  _const_spec(wb.shape), _const_spec(wc.shape), _const_spec(d_skip.shape),
                  _const_spec(wglu.shape), _const_spec(lvl.shape), _const_spec(pw.shape)],
        out_specs=[pl.BlockSpec((tm, d_model), lambda i, f: (i, 0)),
                   pl.BlockSpec((chunks, 2 * n_state), lambda i, f: (i, 0))],
        scratch_shapes=[pltpu.VMEM((tm, 2 * n_state), F32),
                        pltpu.VMEM((1, n_state), F32), pltpu.VMEM((1, n_state), F32)],
    )
    return pl.pallas_call(
        functools.partial(_ssm_kernel, n_state=n_state, d_model=d_model),
        grid_spec=grid_spec,
        out_shape=[jax.ShapeDtypeStruct((t, d_model), BF16),
                   jax.ShapeDtypeStruct((t // CHUNK, 2 * n_state), F32)],
        compiler_params=pltpu.CompilerParams(dimension_semantics=("arbitrary",),
                                             vmem_limit_bytes=VMEM_LIMIT),
        name="ssm",
    )(flags, u, h0, wb, wc, d_skip, wglu, lvl, pw)


def _mix_kernel(flags_ref, v_ref, hist_ref, cw_ref, cb_ref, lg_ref, lb_ref, wco_ref,
                gate_ref, a_ref, x_ref, wout_ref, x1_ref, win_ref, act_ref, *, d_model):
    tile = v_ref.shape[0]
    chunks = tile // CHUNK
    step = pl.program_id(0)
    lead = HIST_ROWS - (CONV_WIDTH - 1)

    @pl.when(step == 0)
    def _():
        win_ref[...] = jnp.zeros_like(win_ref)

    def chunk_body(j, carry):
        r0 = pl.multiple_of(j * CHUNK, CHUNK)
        start = flags_ref[step * chunks + j] != 0
        win_ref[:HIST_ROWS] = jnp.where(start, hist_ref[j], win_ref[CHUNK:])
        win_ref[HIST_ROWS:] = v_ref[pl.ds(r0, CHUNK), :]
        acc = cw_ref[0:1, :] * win_ref[lead:lead + CHUNK, :]
        for k in range(1, CONV_WIDTH):
            acc = acc + cw_ref[k:k + 1, :] * win_ref[lead + k:lead + k + CHUNK, :]
        acc = acc + cb_ref[...]
        mu = jnp.mean(acc, axis=-1, keepdims=True)
        xc = acc - mu
        ln = (xc * lax.rsqrt(jnp.mean(xc * xc, axis=-1, keepdims=True) + EPS)) * lg_ref[...] + lb_ref[...]
        act_ref[pl.ds(r0, CHUNK), :] = (ln * _sigmoid(ln)).astype(BF16)
        return carry

    lax.fori_loop(0, chunks, chunk_body, 0)
    branch_b = jnp.dot(act_ref[...], wco_ref[...], preferred_element_type=F32)
    ga = gate_ref[:, :d_model].astype(F32)
    gb = gate_ref[:, d_model:].astype(F32)
    m = ga * a_ref[...].astype(F32) + gb * branch_b
    x1_ref[...] = x_ref[...] + jnp.dot(m.astype(BF16), wout_ref[...], preferred_element_type=F32)


def _mix(flags, v, hist, conv_w, conv_b, ln_g, ln_b, w_conv_out, gate, branch_a, x, w_out):
    t, d_conv = v.shape
    d_model = x.shape[1]
    tm = TOKEN_TILE
    chunks = tm // CHUNK
    grid_spec = pltpu.PrefetchScalarGridSpec(
        num_scalar_prefetch=1,
        grid=(t // tm,),
        in_specs=[pl.BlockSpec((tm, d_conv), lambda i, f: (i, 0)),
                  pl.BlockSpec((chunks, HIST_ROWS, d_conv), lambda i, f: (i, 0, 0)),
                  _const_spec(conv_w.shape), _const_spec(conv_b.shape), _const_spec(ln_g.shape),
                  _const_spec(ln_b.shape), _const_spec(w_conv_out.shape),
                  pl.BlockSpec((tm, 2 * d_model), lambda i, f: (i, 0)),
                  pl.BlockSpec((tm, d_model), lambda i, f: (i, 0)),
                  pl.BlockSpec((tm, d_model), lambda i, f: (i, 0)),
                  _const_spec(w_out.shape)],
        out_specs=pl.BlockSpec((tm, d_model), lambda i, f: (i, 0)),
        scratch_shapes=[pltpu.VMEM((HIST_ROWS + CHUNK, d_conv), F32),
                        pltpu.VMEM((tm, d_conv), BF16)],
    )
    return pl.pallas_call(
        functools.partial(_mix_kernel, d_model=d_model),
        grid_spec=grid_spec,
        out_shape=jax.ShapeDtypeStruct((t, d_model), F32),
        compiler_params=pltpu.CompilerParams(dimension_semantics=("arbitrary",),
                                             vmem_limit_bytes=VMEM_LIMIT),
        name="mix",
    )(flags, v, hist, conv_w, conv_b, ln_g, ln_b, w_conv_out, gate, branch_a, x, w_out)


def _top16(s, iota):
    iota16 = iota[:PEER_TOPK]

    def rnd(r, carry):
        work, rank, vals = carry
        rf = r.astype(F32)
        m = jnp.max(work, axis=0, keepdims=True)
        first = jnp.min(jnp.where(work == m, iota, float(N_KEYS)), axis=0, keepdims=True)
        sel = iota == first
        rank = jnp.where(sel, rf, rank)
        work = jnp.where(sel, NEG_INF, work)
        vals = jnp.where(iota16 == rf, m, vals)
        return work, rank, vals

    init = (s, jnp.full(s.shape, float(PEER_TOPK), F32), jnp.zeros((PEER_TOPK, s.shape[1]), F32))
    _, rank, vals = lax.fori_loop(0, PEER_TOPK, rnd, init)
    return vals, rank


def _merge16(v1, v2, iota16):
    m0 = v1[0:1] + v2[0:1]

    def rnd(_, carry):
        n, front, z = carry
        m = jnp.max(front, axis=0, keepdims=True)
        a_sel = jnp.min(jnp.where(front == m, iota16, float(PEER_TOPK)), axis=0, keepdims=True)
        sel = iota16 == a_sel
        n = jnp.where(sel, n + 1.0, n)
        k = jnp.max(jnp.where(sel, n, 0.0), axis=0, keepdims=True)
        v2k = jnp.max(jnp.where(iota16 == k, v2, NEG_INF), axis=0, keepdims=True)
        front = jnp.where(sel, v1 + v2k, front)
        return n, front, z + jnp.exp(m - m0)

    init = (jnp.zeros_like(v1), v1 + v2[0:1], jnp.zeros_like(m0))
    n, _, z = lax.fori_loop(0, PEER_TOPK, rnd, init)
    return n, z


def _batcher_network(n):
    def merge(lo, hi, r):
        step = r * 2
        if step < hi - lo:
            yield from merge(lo, hi, step)
            yield from merge(lo + r, hi, step)
            yield from ((i, i + r) for i in range(lo + r, hi - r, step))
        else:
            yield (lo, lo + r)

    def sort(lo, hi):
        if hi - lo >= 1:
            mid = lo + (hi - lo) // 2
            yield from sort(lo, mid)
            yield from sort(mid + 1, hi)
            yield from merge(lo, hi, 1)

    return tuple(sort(0, n - 1))


_SORT16 = _batcher_network(PEER_TOPK)


def _cmpx(x, i, j):
    if x[j] is None:
        return
    if x[i] is None:
        x[i], x[j] = x[j], None
        return
    x[i], x[j] = jnp.maximum(x[i], x[j]), jnp.minimum(x[i], x[j])


def _top16_of_sublane_lists(x):
    n = PEER_TOPK
    for shift in (4, 2, 1):
        y = [None if v is None else pltpu.roll(v, shift, 0) for v in x]
        t = []
        for r in range(n):
            a, b = x[r], y[n - 1 - r]
            t.append(b if a is None else a if b is None else jnp.maximum(a, b))
        for d in (8, 4, 2, 1):
            for r in range(n):
                if r & d == 0:
                    _cmpx(t, r, r | d)
        x = t
    return x


def _sorted_top16(s):
    x = [s[v * SUBLANES:(v + 1) * SUBLANES] for v in range(s.shape[0] // SUBLANES)]
    for i, j in _SORT16:
        _cmpx(x, i, j)
    return _top16_of_sublane_lists(x)


def _count_ge(tiles, bound):
    cnt = jnp.where(tiles[0] >= bound, 1.0, 0.0)
    for v in tiles[1:]:
        cnt = cnt + jnp.where(v >= bound, 1.0, 0.0)
    return jnp.sum(cnt, axis=0, keepdims=True)


def _route_by_value(s1, s2):
    v1 = _sorted_top16(s1)
    v2 = _sorted_top16(s2)
    sub = lax.broadcasted_iota(jnp.int32, (SUBLANES, LANES), 0)

    def by_sublane(vs):
        out = vs[0]
        for r in range(1, SUBLANES):
            out = jnp.where(sub == r, vs[r], out)
        return out

    w1lo, w1hi = by_sublane(v1[:SUBLANES]), by_sublane(v1[SUBLANES:])
    w2lo, w2hi = by_sublane(v2[:SUBLANES]), by_sublane(v2[SUBLANES:])
    cands = [v1[0] + w2lo, v1[0] + w2hi, v1[1] + w2lo,
             jnp.where(sub >= 2, w1lo + v2[0], NEG_INF), w1hi + v2[0],
             jnp.where(sub >= 2, w1lo + v2[1], NEG_INF),
             jnp.where((sub >= 2) & (sub <= 4), v1[2] + w2lo, NEG_INF),
             jnp.where((sub >= 2) & (sub <= 3), v1[3] + w2lo, NEG_INF),
             jnp.where(sub == 2, v1[4] + w2lo, NEG_INF)]
    x = list(cands) + [None] * (PEER_TOPK - len(cands))
    for i, j in _SORT16:
        _cmpx(x, i, j)
    top = _top16_of_sublane_lists(x)
    tau = top[PEER_TOPK - 1]
    z = jnp.exp(top[1] - top[0]) + 1.0
    for r in range(2, PEER_TOPK):
        z = z + jnp.exp(top[r] - top[0])

    s1_tiles = [s1[v * SUBLANES:(v + 1) * SUBLANES] for v in range(N_KEYS // SUBLANES)]
    s2_tiles = [s2[v * SUBLANES:(v + 1) * SUBLANES] for v in range(N_KEYS // SUBLANES)]
    k = float(PEER_TOPK)
    tied = ((_count_ge(s1_tiles, v1[PEER_TOPK - 1]) != k) | (_count_ge(s2_tiles, v2[PEER_TOPK - 1]) != k)
            | (_count_ge(cands, tau) != k))

    n = []
    for a in range(PEER_TOPK):
        na = jnp.where(v1[a] + v2[0] >= tau, 1.0, 0.0)
        for b in range(1, PEER_TOPK // (a + 1)):
            na = na + jnp.where(v1[a] + v2[b] >= tau, 1.0, 0.0)
        n.append(na[0:1])
    n.append(jnp.zeros_like(n[0]))
    key2 = jnp.where(s2 >= v2[0][0:1], 1.0, 0.0)
    for b in range(1, PEER_TOPK):
        key2 = key2 + jnp.where(s2 >= v2[b][0:1], 1.0, 0.0)
    cnt = jnp.where(s1 >= v1[0][0:1], n[0] - n[1], 0.0)
    for a in range(1, PEER_TOPK):
        cnt = cnt + jnp.where(s1 >= v1[a][0:1], n[a] - n[a + 1], 0.0)
    thr = (k + 1.0) - cnt
    p1 = jnp.exp(s1 - v1[0][0:1]) * (1.0 / z[0:1])
    p2 = jnp.exp(s2 - v2[0][0:1])
    return key2, thr, p1, p2, jnp.where(tied, 1.0, 0.0)


def _query_kernel(x1_ref, g_ref, wq_ref, k1_ref, k2_ref,
                  xn_ref, key2_ref, thr_ref, p1_ref, p2_ref, s1_ref, s2_ref, tied_ref):
    tile = x1_ref.shape[0]
    lane_tiles = tile // LANES
    half = k1_ref.shape[1]
    xn = _rms(x1_ref[...], g_ref[...]).astype(BF16)
    xn_ref[...] = xn
    q = jnp.dot(xn, wq_ref[...], preferred_element_type=F32).astype(BF16)
    nt = (((1,), (1,)), ((), ()))
    for h in range(PEER_HEADS):
        q1 = q[:, 2 * h * half:(2 * h + 1) * half]
        q2 = q[:, (2 * h + 1) * half:(2 * h + 2) * half]
        s1 = lax.dot_general(k1_ref[...], q1, nt, preferred_element_type=F32)
        s2 = lax.dot_general(k2_ref[...], q2, nt, preferred_element_type=F32)
        for c in range(lane_tiles):
            s1_ref[h, c] = s1[:, c * LANES:(c + 1) * LANES]
            s2_ref[h, c] = s2[:, c * LANES:(c + 1) * LANES]

    tied_ref[...] = jnp.zeros_like(tied_ref)

    def route_by_value(idx, carry):
        h = idx // lane_tiles
        c = idx % lane_tiles
        key2, thr, p1, p2, tied = _route_by_value(s1_ref[h, c], s2_ref[h, c])
        key2_ref[h, c] = key2
        thr_ref[h, c] = thr
        p1_ref[h, c] = p1
        p2_ref[h, c] = p2
        tied_ref[...] = jnp.maximum(tied_ref[...], tied)
        return carry

    lax.fori_loop(0, PEER_HEADS * lane_tiles, route_by_value, 0)

    def route_by_rank(idx, carry):
        h = idx // lane_tiles
        c = idx % lane_tiles
        iota = lax.broadcasted_iota(jnp.int32, (N_KEYS, LANES), 0).astype(F32)
        s1 = s1_ref[h, c]
        s2 = s2_ref[h, c]
        v1, rank1 = _top16(s1, iota)
        v2, rank2 = _top16(s2, iota)
        n, z = _merge16(v1, v2, iota[:PEER_TOPK])
        cnt = jnp.zeros_like(rank1)
        for a in range(PEER_TOPK):
            cnt = jnp.where(rank1 == float(a), n[a:a + 1], cnt)
        key2_ref[h, c] = float(PEER_TOPK) - rank2
        thr_ref[h, c] = float(PEER_TOPK + 1) - cnt
        p1_ref[h, c] = jnp.exp(s1 - v1[0:1]) / z
        p2_ref[h, c] = jnp.exp(s2 - v2[0:1])
        return carry

    @pl.when(jnp.max(tied_ref[...]) > 0.0)
    def _():
        lax.fori_loop(0, PEER_HEADS * lane_tiles, route_by_rank, 0)


def _query(x1, g, w_q, keys1, keys2):
    t, d = x1.shape
    tm = TOKEN_TILE
    lane_tiles = tm // LANES
    tab_block = (PEER_HEADS, lane_tiles, N_KEYS, LANES)
    tab_spec = pl.BlockSpec(tab_block, lambda i: (0, i, 0, 0))
    tab_shape = jax.ShapeDtypeStruct((PEER_HEADS, t // LANES, N_KEYS, LANES), F32)
    return pl.pallas_call(
        _query_kernel,
        grid=(t // tm,),
        in_specs=[pl.BlockSpec((tm, d), lambda i: (i, 0)), _const_spec(g.shape),
                  _const_spec(w_q.shape), _const_spec(keys1.shape), _const_spec(keys2.shape)],
        out_specs=[pl.BlockSpec((tm, d), lambda i: (i, 0)), tab_spec, tab_spec, tab_spec, tab_spec],
        out_shape=[jax.ShapeDtypeStruct((t, d), BF16), tab_shape, tab_shape, tab_shape, tab_shape],
        scratch_shapes=[pltpu.VMEM(tab_block, F32), pltpu.VMEM(tab_block, F32),
                        pltpu.VMEM((1, LANES), F32)],
        compiler_params=pltpu.CompilerParams(dimension_semantics=("arbitrary",),
                                             vmem_limit_bytes=VMEM_LIMIT),
        name="query",
    )(x1, g, w_q, keys1, keys2)


def _peer_kernel(xn_ref, u_ref, vt_ref, key2_ref, thr_ref, p1_ref, p2_ref, x1_ref, g_ref,
                 y_ref, acc_ref, act0_ref, act1_ref, coef0_ref, coef1_ref):
    s = pl.program_id(1)
    last = pl.num_programs(1) - 1
    eb = u_ref.shape[0] // 2
    tt = xn_ref.shape[0]
    half = tt // 2
    lane_tiles = tt // LANES
    rows = eb // N_KEYS
    nt = (((1,), (1,)), ((), ()))
    act_refs = (act0_ref, act1_ref)
    coef_refs = (coef0_ref, coef1_ref)

    def tail_zero(r):
        return jnp.minimum(jnp.abs(r[r.shape[0] - JROWS:, r.shape[1] - LANES:]), 0.0)

    def stage_a(slot, n):
        r = lax.dot_general(u_ref[slot * eb:(slot + 1) * eb, :], xn_ref[n * half:(n + 1) * half, :], nt,
                            preferred_element_type=F32)
        act_refs[slot][:, n * half:(n + 1) * half] = r
        return tail_zero(r)

    def stage_b(slot, k, c, zero):
        act_ref, coef_ref = act_refs[slot], coef_refs[slot]
        for j0 in range(0, N_KEYS, JROWS):
            ws = [zero] * rows
            for h in range(PEER_HEADS):
                k2 = key2_ref[h, c, j0:j0 + JROWS, :]
                p2 = p2_ref[h, c, j0:j0 + JROWS, :]
                for il in range(rows):
                    i = k * rows + il
                    thr = thr_ref[h, c, pl.ds(i, 1), :]
                    p1 = p1_ref[h, c, pl.ds(i, 1), :]
                    ws[il] = ws[il] + jnp.where(k2 >= thr, p2, 0.0) * p1
            for il in range(rows):
                blk = (slice(il * N_KEYS + j0, il * N_KEYS + j0 + JROWS), slice(c * LANES, (c + 1) * LANES))
                coef_ref[blk] = (ws[il] * _gelu(act_ref[blk])).astype(BF16)

    def stage_c(slot, n):
        cols = slice(n * half, (n + 1) * half)
        r = acc_ref[:, cols] + jnp.dot(vt_ref[:, slot * eb:(slot + 1) * eb], coef_refs[slot][:, cols],
                                       preferred_element_type=F32)
        acc_ref[:, cols] = r
        return tail_zero(r)

    zero0 = jnp.zeros((JROWS, LANES), F32)

    @pl.when(s == 0)
    def _():
        acc_ref[...] = jnp.zeros_like(acc_ref)
        z = stage_a(0, 0)
        stage_a(0, 1)
        stage_b(0, 0, 0, z)
        z = stage_a(1, 0)
        stage_b(0, 0, 1, z)
        z = stage_a(1, 1)
        stage_b(0, 0, 2, z)
        stage_b(0, 0, 3, z)

    @pl.when(jnp.logical_and(s > 0, s < last))
    def _():
        ka, kb = 2 * s - 1, 2 * s
        z = stage_a(0, 0)
        stage_b(1, ka, 0, zero0)
        z2 = stage_a(0, 1)
        stage_b(1, ka, 1, z)
        z = stage_c(0, 0)
        stage_b(1, ka, 2, z2)
        z2 = stage_c(0, 1)
        stage_b(1, ka, 3, z)
        z = stage_a(1, 0)
        stage_b(0, kb, 0, z2)
        z2 = stage_a(1, 1)
        stage_b(0, kb, 1, z)
        z = stage_c(1, 0)
        stage_b(0, kb, 2, z2)
        stage_c(1, 1)
        stage_b(0, kb, 3, z)

    @pl.when(s == last)
    def _():
        ka = 2 * s - 1
        z = stage_c(0, 0)
        stage_b(1, ka, 0, zero0)
        stage_b(1, ka, 1, z)
        z = stage_c(0, 1)
        stage_b(1, ka, 2, z)
        stage_b(1, ka, 3, z)
        stage_c(1, 0)
        stage_c(1, 1)
        y_ref[...] = _rms(x1_ref[...] + acc_ref[...].T, g_ref[...])


def _peer(xn, u_tab, vt_tab, key2, thr, p1, p2, x1, g):
    t, d = xn.shape
    n_exp = u_tab.shape[0]
    tt, eb = TOKEN_TILE, EXPERT_BLOCK
    lane_tiles = tt // LANES
    n_pairs = n_exp // (2 * eb)
    tab_spec = pl.BlockSpec((PEER_HEADS, lane_tiles, N_KEYS, LANES), lambda i, s: (0, i, 0, 0))
    return pl.pallas_call(
        _peer_kernel,
        grid=(t // tt, n_pairs + 1),
        in_specs=[pl.BlockSpec((tt, d), lambda i, s: (i, 0)),
                  pl.BlockSpec((2 * eb, d), lambda i, s: (jnp.minimum(s, n_pairs - 1), 0)),
                  pl.BlockSpec((d, 2 * eb), lambda i, s: (0, jnp.maximum(s - 1, 0))),
                  tab_spec, tab_spec, tab_spec, tab_spec,
                  pl.BlockSpec((tt, d), lambda i, s: (i, 0)),
                  pl.BlockSpec((1, d), lambda i, s: (0, 0))],
        out_specs=pl.BlockSpec((tt, d), lambda i, s: (i, 0)),
        out_shape=jax.ShapeDtypeStruct((t, d), F32),
        scratch_shapes=[pltpu.VMEM((d, tt), F32), pltpu.VMEM((eb, tt), F32), pltpu.VMEM((eb, tt), F32),
                        pltpu.VMEM((eb, tt), BF16), pltpu.VMEM((eb, tt), BF16)],
        compiler_params=pltpu.CompilerParams(dimension_semantics=("arbitrary", "arbitrary"),
                                             vmem_limit_bytes=VMEM_LIMIT),
        name="peer",
    )(xn, u_tab, vt_tab, key2, thr, p1, p2, x1, g)


def _ssm_params(lam_re, lam_im, log_step, b_re, b_im, c_re, c_im, d_skip):
    g, p = lam_re.shape
    hh = b_re.shape[2]
    step = jnp.exp(log_step)[:, None]
    decay = jnp.exp(lam_re * step)
    ab_re = decay * jnp.cos(lam_im * step)
    ab_im = decay * jnp.sin(lam_im * step)
    inv = 1.0 / (lam_re * lam_re + lam_im * lam_im)
    k_re = ((ab_re - 1.0) * lam_re + ab_im * lam_im) * inv
    k_im = (ab_im * lam_re - (ab_re - 1.0) * lam_im) * inv
    bb_re = k_re[..., None] * b_re - k_im[..., None] * b_im
    bb_im = k_re[..., None] * b_im + k_im[..., None] * b_re
    eye = jnp.eye(g, dtype=F32)

    def in_mat(bb):
        return jnp.einsum("gph,gk->ghkp", bb, eye).reshape(g * hh, g * p)

    def out_mat(cc):
        return jnp.einsum("ghp,gk->gpkh", cc, eye).reshape(g * p, g * hh)

    wb = jnp.concatenate([in_mat(bb_re), in_mat(bb_im)], axis=1).astype(BF16)
    wc = jnp.concatenate([out_mat(c_re), -out_mat(c_im)], axis=0).astype(BF16)

    a_re = ab_re.reshape(1, g * p)
    a_im = ab_im.reshape(1, g * p)
    pows = [(a_re, a_im)]
    for _ in range(SUBLANES - 1):
        q_re, q_im = pows[-1]
        pows.append((q_re * a_re - q_im * a_im, q_re * a_im + q_im * a_re))
    row = jnp.arange(SUBLANES)[:, None]
    lvl = jnp.stack([jnp.stack([jnp.where(row >= k, pows[k - 1][0], 0.0),
                                jnp.where(row >= k, pows[k - 1][1], 0.0)]) for k in (1, 2, 4)])
    pw = jnp.stack([jnp.concatenate([q[0] for q in pows], axis=0),
                    jnp.concatenate([q[1] for q in pows], axis=0)])
    return wb, wc, d_skip.reshape(1, g * hh), lvl, pw


def kernel(x_prompt, x_sample, state_ssm_re, state_ssm_im, cache_conv, norm_mix, w_in, b_gate,
           ssm_lambda_re, ssm_lambda_im, ssm_log_step, ssm_b_re, ssm_b_im, ssm_c_re, ssm_c_im,
           ssm_d, w_ssm_glu, conv_w, conv_b, conv_ln_g, conv_ln_b, w_conv_out, w_out, norm_ffn,
           peer_w_q, peer_keys1, peer_keys2, peer_u, peer_v, norm_final):
    depth = w_in.shape[0]
    assert depth == 1, "single-layer stack only"
    n_p, l_p, d = x_prompt.shape
    n_s, l_s, _ = x_sample.shape
    assert l_p % CHUNK == 0 and l_s % CHUNK == 0
    t_p, t_s = n_p * l_p, n_s * l_s
    t = t_p + t_s
    assert t % TOKEN_TILE == 0
    g, p = ssm_lambda_re.shape[1:]
    n_state = g * p
    d_ssm = g * ssm_b_re.shape[3]
    d_conv = conv_w.shape[2]
    hist_len = CONV_WIDTH - 1

    x = jnp.concatenate([x_prompt.reshape(t_p, d), x_sample.reshape(t_s, d)], axis=0)

    cp, cs = l_p // CHUNK, l_s // CHUNK
    flag_p = jnp.zeros((n_p, cp), jnp.int32).at[:, 0].set(1).reshape(-1)
    flag_s = jnp.zeros((n_s, cs), jnp.int32).at[:, 0].set(1).reshape(-1)
    flags = jnp.concatenate([flag_p, flag_s])
    h0_s = jnp.concatenate([state_ssm_re[0].reshape(n_s, n_state), state_ssm_im[0].reshape(n_s, n_state)], axis=1)
    h0_s = jnp.zeros((n_s, cs, 2 * n_state), F32).at[:, 0].set(h0_s).reshape(n_s * cs, 2 * n_state)
    h0 = jnp.concatenate([jnp.zeros((n_p * cp, 2 * n_state), F32), h0_s], axis=0)
    hist_s = jnp.pad(cache_conv[0], ((0, 0), (HIST_ROWS - hist_len, 0), (0, 0)))
    hist_s = jnp.zeros((n_s, cs, HIST_ROWS, d_conv), F32).at[:, 0].set(hist_s).reshape(n_s * cs, HIST_ROWS, d_conv)
    hist = jnp.concatenate([jnp.zeros((n_p * cp, HIST_ROWS, d_conv), F32), hist_s], axis=0)

    wb, wc, d_skip, lvl, pw = _ssm_params(ssm_lambda_re[0], ssm_lambda_im[0], ssm_log_step[0],
                                          ssm_b_re[0], ssm_b_im[0], ssm_c_re[0], ssm_c_im[0], ssm_d[0])

    u, v, gate = _in_proj(x, norm_mix, w_in[0].astype(BF16), b_gate, d_ssm, d_conv)
    branch_a, hend = _ssm(flags, u, h0, wb, wc, d_skip, w_ssm_glu[0].astype(BF16), lvl, pw, d)
    cw = jnp.pad(conv_w[0], ((0, HIST_ROWS - CONV_WIDTH), (0, 0)))
    x1 = _mix(flags, v, hist, cw, conv_b, conv_ln_g, conv_ln_b, w_conv_out[0].astype(BF16),
              gate, branch_a, x, w_out[0].astype(BF16))
    xn2, key2, thr, p1, p2 = _query(x1, norm_ffn, peer_w_q[0].astype(BF16),
                                    peer_keys1[0].astype(BF16), peer_keys2[0].astype(BF16))
    y = _peer(xn2, peer_u[0].astype(BF16), peer_v[0].T.astype(BF16), key2, thr, p1, p2, x1,
              norm_final.reshape(1, d))

    y_prompt = y[:t_p].reshape(n_p, l_p, d)
    y_sample = y[t_p:].reshape(n_s, l_s, d)
    last_p = hend[:n_p * cp].reshape(n_p, cp, 2 * n_state)[:, -1]
    last_s = hend[n_p * cp:].reshape(n_s, cs, 2 * n_state)[:, -1]
    v_p = v[:t_p].reshape(n_p, l_p, d_conv)[:, l_p - hist_len:]
    v_s = v[t_p:].reshape(n_s, l_s, d_conv)[:, l_s - hist_len:]
    return (y_prompt, y_sample,
            last_p[:, :n_state].reshape(1, n_p, g, p), last_p[:, n_state:].reshape(1, n_p, g, p), v_p[None],
            last_s[:, :n_state].reshape(1, n_s, g, p), last_s[:, n_state:].reshape(1, n_s, g, p), v_s[None])
```

```python
import functools
import math

import jax
import jax.numpy as jnp
from jax import lax
from jax.experimental import pallas as pl
from jax.experimental.pallas import tpu as pltpu

F32 = jnp.float32
BF16 = jnp.bfloat16

EPS = 1e-6
CHUNK = 64
SSM_GROUP = 16
SSM_STATE = 64
CONV_WIDTH = 31
HIST_ROWS = 32
N_KEYS = 128
PEER_HEADS = 8
PEER_TOPK = 16
LANES = 128
SUBLANES = 8
TOKEN_TILE = 512
EXPERT_BLOCK = 1024
JROWS = 64
VMEM_LIMIT = 58 * 1024 * 1024
NEG_INF = float("-inf")


def _rms(x, g):
    ms = jnp.mean(x * x, axis=-1, keepdims=True)
    return (x * lax.rsqrt(ms + EPS)) * g


def _gelu(x):
    return 0.5 * x * (1.0 + lax.erf(x * (1.0 / math.sqrt(2.0))))


def _sigmoid(x):
    return 1.0 / (1.0 + jnp.exp(-x))


def _const_spec(shape):
    nd = len(shape)
    return pl.BlockSpec(shape, lambda *_: (0,) * nd)


def _stream_specs(tm, d, tiles_p, pipeline_mode=None):
    return [pl.BlockSpec((tm, d), lambda i, *_: (jnp.minimum(i, tiles_p - 1), 0), pipeline_mode=pipeline_mode),
            pl.BlockSpec((tm, d), lambda i, *_: (jnp.maximum(i - tiles_p, 0), 0), pipeline_mode=pipeline_mode)]


def _in_proj_kernel(xp_ref, xs_ref, g_ref, w_ref, bg_ref, u_ref, v_ref, gate_ref, *, d_ssm, d_conv, tiles_p):
    x = jnp.where(pl.program_id(0) < tiles_p, xp_ref[...], xs_ref[...])
    xn = _rms(x, g_ref[...]).astype(BF16)
    u_ref[...] = jnp.dot(xn, w_ref[:, :d_ssm], preferred_element_type=F32)
    pq = jnp.dot(xn, w_ref[:, d_ssm:d_ssm + 2 * d_conv], preferred_element_type=F32)
    v_ref[...] = pq[:, :d_conv] * _sigmoid(pq[:, d_conv:])
    gl = jnp.dot(xn, w_ref[:, d_ssm + 2 * d_conv:], preferred_element_type=F32)
    gate_ref[...] = _sigmoid(gl + bg_ref[...]).astype(BF16)


def _in_proj(xp, xs, g, w_in, b_gate, d_ssm, d_conv):
    d = xp.shape[1]
    t = xp.shape[0] + xs.shape[0]
    d_in = w_in.shape[1]
    n_gate = d_in - d_ssm - 2 * d_conv
    tm = TOKEN_TILE
    tiles_p = xp.shape[0] // tm
    return pl.pallas_call(
        functools.partial(_in_proj_kernel, d_ssm=d_ssm, d_conv=d_conv, tiles_p=tiles_p),
        grid=(t // tm,),
        in_specs=_stream_specs(tm, d, tiles_p) +
                 [_const_spec((1, d)), _const_spec((d, d_in)), _const_spec((1, n_gate))],
        out_specs=[pl.BlockSpec((tm, d_ssm), lambda i: (i, 0)),
                   pl.BlockSpec((tm, d_conv), lambda i: (i, 0)),
                   pl.BlockSpec((tm, n_gate), lambda i: (i, 0))],
        out_shape=[jax.ShapeDtypeStruct((t, d_ssm), F32),
                   jax.ShapeDtypeStruct((t, d_conv), F32),
                   jax.ShapeDtypeStruct((t, n_gate), BF16)],
        compiler_params=pltpu.CompilerParams(dimension_semantics=("arbitrary",),
                                             vmem_limit_bytes=VMEM_LIMIT),
        name="in_proj",
    )(xp, xs, g, w_in, b_gate)


def _ssm_kernel(seq_ref, u_ref, h0_ref, wb_ref, wc_ref, d_ref, wglu_ref, lvl_ref, pw_ref,
                a_ref, hend_ref, bu_ref, cre_ref, cim_ref, *, n_state, d_model):
    tile = u_ref.shape[0]
    chunks = tile // CHUNK
    step = pl.program_id(0)

    @pl.when(step == 0)
    def _():
        cre_ref[...] = jnp.zeros_like(cre_ref)
        cim_ref[...] = jnp.zeros_like(cim_ref)

    u = u_ref[...]
    bu_ref[...] = jnp.dot(u.astype(BF16), wb_ref[...], preferred_element_type=F32)

    for j in range(chunks):
        seq = seq_ref[step * chunks + j]
        start = seq >= 0
        row = jnp.maximum(seq, 0)
        c_re = jnp.where(start, h0_ref[pl.ds(row, 1), :n_state], cre_ref[...])
        c_im = jnp.where(start, h0_ref[pl.ds(row, 1), n_state:], cim_ref[...])

        def block(b, carry, j=j):
            c_re, c_im = carry
            r0 = pl.multiple_of((j * (CHUNK // SUBLANES) + b) * SUBLANES, SUBLANES)
            x_re = bu_ref[pl.ds(r0, SUBLANES), :n_state]
            x_im = bu_ref[pl.ds(r0, SUBLANES), n_state:]
            for lvl, k in enumerate((1, 2, 4)):
                m_re = lvl_ref[lvl, 0]
                m_im = lvl_ref[lvl, 1]
                t_re = pltpu.roll(x_re, k, 0)
                t_im = pltpu.roll(x_im, k, 0)
                x_re, x_im = (x_re + m_re * t_re - m_im * t_im,
                              x_im + m_re * t_im + m_im * t_re)
            p_re = pw_ref[0]
            p_im = pw_ref[1]
            h_re = x_re + p_re * c_re - p_im * c_im
            h_im = x_im + p_re * c_im + p_im * c_re
            bu_ref[pl.ds(r0, SUBLANES), :n_state] = h_re
            bu_ref[pl.ds(r0, SUBLANES), n_state:] = h_im
            return h_re[SUBLANES - 1:], h_im[SUBLANES - 1:]

        c_re, c_im = lax.fori_loop(0, CHUNK // SUBLANES, block, (c_re, c_im))
        cre_ref[...] = c_re
        cim_ref[...] = c_im
        hend_ref[j:j + 1, :n_state] = c_re
        hend_ref[j:j + 1, n_state:] = c_im

    y = jnp.dot(bu_ref[...].astype(BF16), wc_ref[...], preferred_element_type=F32) + d_ref[...] * u
    pq = jnp.dot(_gelu(y).astype(BF16), wglu_ref[...], preferred_element_type=F32)
    a_ref[...] = (pq[:, :d_model] * _sigmoid(pq[:, d_model:])).astype(BF16)


def _ssm(seq, u, h0, wb, wc, d_skip, wglu, lvl, pw, d_model):
    t, d_ssm = u.shape
    n_state = wb.shape[1] // 2
    tm = TOKEN_TILE
    chunks = tm // CHUNK
    grid_spec = pltpu.PrefetchScalarGridSpec(
        num_scalar_prefetch=1,
        grid=(t // tm,),
        in_specs=[pl.BlockSpec((tm, d_ssm), lambda i, f: (i, 0)),
                  _const_spec(h0.shape),
                  _const_spec(wb.shape), _const_spec(wc.shape), _const_spec(d_skip.shape),
                  _const_spec(wglu.shape), _const_spec(lvl.shape), _const_spec(pw.shape)],
        out_specs=[pl.BlockSpec((tm, d_model), lambda i, f: (i, 0)),
                   pl.BlockSpec((chunks, 2 * n_state), lambda i, f: (i, 0))],
        scratch_shapes=[pltpu.VMEM((tm, 2 * n_state), F32),
                        pltpu.VMEM((1, n_state), F32), pltpu.VMEM((1, n_state), F32)],
    )
    return pl.pallas_call(
        functools.partial(_ssm_kernel, n_state=n_state, d_model=d_model),
        grid_spec=grid_spec,
        out_shape=[jax.ShapeDtypeStruct((t, d_model), BF16),
                   jax.ShapeDtypeStruct((t // CHUNK, 2 * n_state), F32)],
        compiler_params=pltpu.CompilerParams(dimension_semantics=("arbitrary",),
                                             vmem_limit_bytes=VMEM_LIMIT),
        name="ssm",
    )(seq, u, h0, wb, wc, d_skip, wglu, lvl, pw)


def _mix_kernel(seq_ref, v_ref, hist_ref, cw_ref, cb_ref, lg_ref, lb_ref, wco_ref,
                gate_ref, a_ref, xp_ref, xs_ref, wout_ref, x1_ref, win_ref, act_ref, *, d_model, tiles_p):
    tile = v_ref.shape[0]
    chunks = tile // CHUNK
    step = pl.program_id(0)
    lead = HIST_ROWS - (CONV_WIDTH - 1)

    @pl.when(step == 0)
    def _():
        win_ref[...] = jnp.zeros_like(win_ref)

    def chunk_body(j, carry):
        r0 = pl.multiple_of(j * CHUNK, CHUNK)
        seq = seq_ref[step * chunks + j]
        win_ref[:HIST_ROWS] = jnp.where(seq >= 0, hist_ref[jnp.maximum(seq, 0)], win_ref[CHUNK:])
        win_ref[HIST_ROWS:] = v_ref[pl.ds(r0, CHUNK), :]
        acc = cw_ref[0:1, :] * win_ref[lead:lead + CHUNK, :]
        for k in range(1, CONV_WIDTH):
            acc = acc + cw_ref[k:k + 1, :] * win_ref[lead + k:lead + k + CHUNK, :]
        acc = acc + cb_ref[...]
        mu = jnp.mean(acc, axis=-1, keepdims=True)
        xc = acc - mu
        ln = (xc * lax.rsqrt(jnp.mean(xc * xc, axis=-1, keepdims=True) + EPS)) * lg_ref[...] + lb_ref[...]
        act_ref[pl.ds(r0, CHUNK), :] = (ln * _sigmoid(ln)).astype(BF16)
        return carry

    lax.fori_loop(0, chunks, chunk_body, 0)
    branch_b = jnp.dot(act_ref[...], wco_ref[...], preferred_element_type=F32)
    ga = gate_ref[:, :d_model].astype(F32)
    gb = gate_ref[:, d_model:].astype(F32)
    m = ga * a_ref[...].astype(F32) + gb * branch_b
    x = jnp.where(step < tiles_p, xp_ref[...], xs_ref[...])
    x1_ref[...] = x + jnp.dot(m.astype(BF16), wout_ref[...], preferred_element_type=F32)


def _mix(seq, v, hist, conv_w, conv_b, ln_g, ln_b, w_conv_out, gate, branch_a, xp, xs, w_out):
    t, d_conv = v.shape
    d_model = xp.shape[1]
    tm = TOKEN_TILE
    tiles_p = xp.shape[0] // tm
    grid_spec = pltpu.PrefetchScalarGridSpec(
        num_scalar_prefetch=1,
        grid=(t // tm,),
        in_specs=[pl.BlockSpec((tm, d_conv), lambda i, f: (i, 0)),
                  _const_spec(hist.shape),
                  _const_spec(conv_w.shape), _const_spec(conv_b.shape), _const_spec(ln_g.shape),
                  _const_spec(ln_b.shape), _const_spec(w_conv_out.shape),
                  pl.BlockSpec((tm, 2 * d_model), lambda i, f: (i, 0)),
                  pl.BlockSpec((tm, d_model), lambda i, f: (i, 0))] +
                 _stream_specs(tm, d_model, tiles_p) + [_const_spec(w_out.shape)],
        out_specs=pl.BlockSpec((tm, d_model), lambda i, f: (i, 0)),
        scratch_shapes=[pltpu.VMEM((HIST_ROWS + CHUNK, d_conv), F32),
                        pltpu.VMEM((tm, d_conv), BF16)],
    )
    return pl.pallas_call(
        functools.partial(_mix_kernel, d_model=d_model, tiles_p=tiles_p),
        grid_spec=grid_spec,
        out_shape=jax.ShapeDtypeStruct((t, d_model), F32),
        compiler_params=pltpu.CompilerParams(dimension_semantics=("arbitrary",),
                                             vmem_limit_bytes=VMEM_LIMIT),
        name="mix",
    )(seq, v, hist, conv_w, conv_b, ln_g, ln_b, w_conv_out, gate, branch_a, xp, xs, w_out)


def _top16(s, iota):
    iota16 = iota[:PEER_TOPK]

    def rnd(r, carry):
        work, rank, vals = carry
        rf = jnp.asarray(r, F32)
        m = jnp.max(work, axis=0, keepdims=True)
        first = jnp.min(jnp.where(work == m, iota, float(N_KEYS)), axis=0, keepdims=True)
        sel = iota == first
        rank = jnp.where(sel, rf, rank)
        work = jnp.where(sel, NEG_INF, work)
        vals = jnp.where(iota16 == rf, m, vals)
        return work, rank, vals

    init = (s, jnp.full(s.shape, float(PEER_TOPK), F32), jnp.zeros((PEER_TOPK, s.shape[1]), F32))
    _, rank, vals = lax.fori_loop(0, PEER_TOPK, rnd, init)
    return vals, rank


def _merge16(v1, v2, iota16):
    m0 = v1[0:1] + v2[0:1]

    def rnd(_, carry):
        n, front, z = carry
        m = jnp.max(front, axis=0, keepdims=True)
        a_sel = jnp.min(jnp.where(front == m, iota16, float(PEER_TOPK)), axis=0, keepdims=True)
        sel = iota16 == a_sel
        n = jnp.where(sel, n + 1.0, n)
        k = jnp.max(jnp.where(sel, n, 0.0), axis=0, keepdims=True)
        v2k = jnp.max(jnp.where(iota16 == k, v2, NEG_INF), axis=0, keepdims=True)
        front = jnp.where(sel, v1 + v2k, front)
        return n, front, z + jnp.exp(m - m0)

    init = (jnp.zeros_like(v1), v1 + v2[0:1], jnp.zeros_like(m0))
    n, _, z = lax.fori_loop(0, PEER_TOPK, rnd, init)
    return n, z


def _batcher_network(n):
    def merge(lo, hi, r):
        step = r * 2
        if step < hi - lo:
            yield from merge(lo, hi, step)
            yield from merge(lo + r, hi, step)
            yield from ((i, i + r) for i in range(lo + r, hi - r, step))
        else:
            yield (lo, lo + r)

    def sort(lo, hi):
        if hi - lo >= 1:
            mid = lo + (hi - lo) // 2
            yield from sort(lo, mid)
            yield from sort(mid + 1, hi)
            yield from merge(lo, hi, 1)

    return tuple(sort(0, n - 1))


_SORT16 = _batcher_network(PEER_TOPK)


def _cmpx(x, i, j):
    if x[j] is None:
        return
    if x[i] is None:
        x[i], x[j] = x[j], None
        return
    x[i], x[j] = jnp.maximum(x[i], x[j]), jnp.minimum(x[i], x[j])


def _top16_of_sublane_lists(x):
    n = PEER_TOPK
    for shift in (4, 2, 1):
        y = [None if v is None else pltpu.roll(v, shift, 0) for v in x]
        t = []
        for r in range(n):
            a, b = x[r], y[n - 1 - r]
            t.append(b if a is None else a if b is None else jnp.maximum(a, b))
        for d in (8, 4, 2, 1):
            for r in range(n):
                if r & d == 0:
                    _cmpx(t, r, r | d)
        x = t
    return x


def _sorted_top16(s):
    x = [s[v * SUBLANES:(v + 1) * SUBLANES] for v in range(s.shape[0] // SUBLANES)]
    for i, j in _SORT16:
        _cmpx(x, i, j)
    return _top16_of_sublane_lists(x)


def _count_ge(tiles, bound):
    cnt = jnp.where(tiles[0] >= bound, 1.0, 0.0)
    for v in tiles[1:]:
        cnt = cnt + jnp.where(v >= bound, 1.0, 0.0)
    return jnp.sum(cnt, axis=0, keepdims=True)


def _route_by_value(s1, s2):
    v1 = _sorted_top16(s1)
    v2 = _sorted_top16(s2)
    sub = lax.broadcasted_iota(jnp.int32, (SUBLANES, LANES), 0)

    def by_sublane(vs):
        out = vs[0]
        for r in range(1, SUBLANES):
            out = jnp.where(sub == r, vs[r], out)
        return out

    w1lo, w1hi = by_sublane(v1[:SUBLANES]), by_sublane(v1[SUBLANES:])
    w2lo, w2hi = by_sublane(v2[:SUBLANES]), by_sublane(v2[SUBLANES:])
    cands = [v1[0] + w2lo, v1[0] + w2hi, v1[1] + w2lo,
             jnp.where(sub >= 2, w1lo + v2[0], NEG_INF), w1hi + v2[0],
             jnp.where(sub >= 2, w1lo + v2[1], NEG_INF),
             jnp.where((sub >= 2) & (sub <= 4), v1[2] + w2lo, NEG_INF),
             jnp.where((sub >= 2) & (sub <= 3), v1[3] + w2lo, NEG_INF),
             jnp.where(sub == 2, v1[4] + w2lo, NEG_INF)]
    x = list(cands) + [None] * (PEER_TOPK - len(cands))
    for i, j in _SORT16:
        _cmpx(x, i, j)
    top = _top16_of_sublane_lists(x)
    tau = top[PEER_TOPK - 1]
    z = jnp.exp(top[1] - top[0]) + 1.0
    for r in range(2, PEER_TOPK):
        z = z + jnp.exp(top[r] - top[0])

    s1_tiles = [s1[v * SUBLANES:(v + 1) * SUBLANES] for v in range(N_KEYS // SUBLANES)]
    s2_tiles = [s2[v * SUBLANES:(v + 1) * SUBLANES] for v in range(N_KEYS // SUBLANES)]
    k = float(PEER_TOPK)
    tied = ((_count_ge(s1_tiles, v1[PEER_TOPK - 1]) != k) | (_count_ge(s2_tiles, v2[PEER_TOPK - 1]) != k)
            | (_count_ge(cands, tau) != k))

    n = []
    for a in range(PEER_TOPK):
        na = jnp.where(v1[a] + v2[0] >= tau, 1.0, 0.0)
        for b in range(1, PEER_TOPK // (a + 1)):
            na = na + jnp.where(v1[a] + v2[b] >= tau, 1.0, 0.0)
        n.append(na[0:1])
    n.append(jnp.zeros_like(n[0]))
    key2 = jnp.where(s2 >= v2[0][0:1], 1.0, 0.0)
    for b in range(1, PEER_TOPK):
        key2 = key2 + jnp.where(s2 >= v2[b][0:1], 1.0, 0.0)
    cnt = jnp.where(s1 >= v1[0][0:1], n[0] - n[1], 0.0)
    for a in range(1, PEER_TOPK):
        cnt = cnt + jnp.where(s1 >= v1[a][0:1], n[a] - n[a + 1], 0.0)
    thr = (k + 1.0) - cnt
    p1 = jnp.exp(s1 - v1[0][0:1]) * (1.0 / z[0:1])
    p2 = jnp.exp(s2 - v2[0][0:1])
    return key2, thr, p1, p2, jnp.where(tied, 1.0, 0.0)


def _query_kernel(x1_ref, g_ref, wq_ref, k1_ref, k2_ref,
                  xn_ref, key2_ref, thr_ref, p1_ref, p2_ref, s1_ref, s2_ref, tied_ref):
    tile = x1_ref.shape[0]
    lane_tiles = tile // LANES
    half = k1_ref.shape[1]
    xn = _rms(x1_ref[...], g_ref[...]).astype(BF16)
    xn_ref[...] = xn
    q = jnp.dot(xn, wq_ref[...], preferred_element_type=F32).astype(BF16)
    nt = (((1,), (1,)), ((), ()))
    for h in range(PEER_HEADS):
        q1 = q[:, 2 * h * half:(2 * h + 1) * half]
        q2 = q[:, (2 * h + 1) * half:(2 * h + 2) * half]
        s1 = lax.dot_general(k1_ref[...], q1, nt, preferred_element_type=F32)
        s2 = lax.dot_general(k2_ref[...], q2, nt, preferred_element_type=F32)
        for c in range(lane_tiles):
            s1_ref[h, c] = s1[:, c * LANES:(c + 1) * LANES]
            s2_ref[h, c] = s2[:, c * LANES:(c + 1) * LANES]

    tied_ref[...] = jnp.zeros_like(tied_ref)

    def route_by_value(idx, carry):
        h = idx // lane_tiles
        c = idx % lane_tiles
        key2, thr, p1, p2, tied = _route_by_value(s1_ref[h, c], s2_ref[h, c])
        key2_ref[h, c] = key2
        thr_ref[h, c] = thr
        p1_ref[h, c] = p1
        p2_ref[h, c] = p2
        tied_ref[...] = jnp.maximum(tied_ref[...], tied)
        return carry

    lax.fori_loop(0, PEER_HEADS * lane_tiles, route_by_value, 0)

    def route_by_rank(idx, carry):
        h = idx // lane_tiles
        c = idx % lane_tiles
        iota = lax.broadcasted_iota(jnp.int32, (N_KEYS, LANES), 0).astype(F32)
        s1 = s1_ref[h, c]
        s2 = s2_ref[h, c]
        v1, rank1 = _top16(s1, iota)
        v2, rank2 = _top16(s2, iota)
        n, z = _merge16(v1, v2, iota[:PEER_TOPK])
        cnt = jnp.zeros_like(rank1)
        for a in range(PEER_TOPK):
            cnt = jnp.where(rank1 == float(a), n[a:a + 1], cnt)
        key2_ref[h, c] = float(PEER_TOPK) - rank2
        thr_ref[h, c] = float(PEER_TOPK + 1) - cnt
        p1_ref[h, c] = jnp.exp(s1 - v1[0:1]) / z
        p2_ref[h, c] = jnp.exp(s2 - v2[0:1])
        return carry

    @pl.when(jnp.max(tied_ref[...]) > 0.0)
    def _():
        lax.fori_loop(0, PEER_HEADS * lane_tiles, route_by_rank, 0)


def _query(x1, g, w_q, keys1, keys2):
    t, d = x1.shape
    tm = TOKEN_TILE
    lane_tiles = tm // LANES
    tab_block = (PEER_HEADS, lane_tiles, N_KEYS, LANES)
    tab_spec = pl.BlockSpec(tab_block, lambda i: (0, i, 0, 0))
    tab_shape = jax.ShapeDtypeStruct((PEER_HEADS, t // LANES, N_KEYS, LANES), F32)
    return pl.pallas_call(
        _query_kernel,
        grid=(t // tm,),
        in_specs=[pl.BlockSpec((tm, d), lambda i: (i, 0)), _const_spec(g.shape),
                  _const_spec(w_q.shape), _const_spec(keys1.shape), _const_spec(keys2.shape)],
        out_specs=[pl.BlockSpec((tm, d), lambda i: (i, 0)), tab_spec, tab_spec, tab_spec, tab_spec],
        out_shape=[jax.ShapeDtypeStruct((t, d), BF16), tab_shape, tab_shape, tab_shape, tab_shape],
        scratch_shapes=[pltpu.VMEM(tab_block, F32), pltpu.VMEM(tab_block, F32),
                        pltpu.VMEM((1, LANES), F32)],
        compiler_params=pltpu.CompilerParams(dimension_semantics=("arbitrary",),
                                             vmem_limit_bytes=VMEM_LIMIT),
        name="query",
    )(x1, g, w_q, keys1, keys2)


def _peer_kernel(xn_ref, u_ref, vt_ref, key2_ref, thr_ref, p1_ref, p2_ref, x1_ref, g_ref,
                 yp_ref, ys_ref, acc_ref, act0_ref, act1_ref, coef0_ref, coef1_ref, *, tiles_p):
    s = pl.program_id(1)
    last = pl.num_programs(1) - 1
    eb = u_ref.shape[0] // 2
    tt = xn_ref.shape[0]
    half = tt // 2
    rows = eb // N_KEYS
    nt = (((1,), (1,)), ((), ()))
    act_refs = (act0_ref, act1_ref)
    coef_refs = (coef0_ref, coef1_ref)

    def tail_zero(r):
        return jnp.minimum(jnp.abs(r[r.shape[0] - JROWS:, r.shape[1] - LANES:]), 0.0)

    def stage_a(slot, n):
        r = lax.dot_general(u_ref[slot * eb:(slot + 1) * eb, :], xn_ref[n * half:(n + 1) * half, :], nt,
                            preferred_element_type=F32)
        act_refs[slot][:, n * half:(n + 1) * half] = r
        return tail_zero(r)

    def stage_b(slot, k, c, zero):
        act_ref, coef_ref = act_refs[slot], coef_refs[slot]
        zero_bf = zero.astype(BF16)
        for j0 in range(0, N_KEYS, JROWS):
            ws = [zero_bf] * rows
            for h in range(PEER_HEADS):
                k2 = key2_ref[h, c, j0:j0 + JROWS, :].astype(BF16)
                p2 = p2_ref[h, c, j0:j0 + JROWS, :].astype(BF16)
                for il in range(rows):
                    i = k * rows + il
                    thr = jnp.broadcast_to(thr_ref[h, c, pl.ds(i, 1), :], (JROWS, LANES)).astype(BF16)
                    p1 = jnp.broadcast_to(p1_ref[h, c, pl.ds(i, 1), :], (JROWS, LANES)).astype(BF16)
                    ws[il] = ws[il] + jnp.where(k2 >= thr, p2, jnp.zeros_like(p2)) * p1
            for il in range(rows):
                blk = (slice(il * N_KEYS + j0, il * N_KEYS + j0 + JROWS), slice(c * LANES, (c + 1) * LANES))
                coef_ref[blk] = ws[il] * _gelu(act_ref[blk]).astype(BF16)

    def stage_c(slot, n):
        cols = slice(n * half, (n + 1) * half)
        r = acc_ref[:, cols] + jnp.dot(vt_ref[:, slot * eb:(slot + 1) * eb], coef_refs[slot][:, cols],
                                       preferred_element_type=F32)
        acc_ref[:, cols] = r
        return tail_zero(r)

    zero0 = jnp.zeros((JROWS, LANES), F32)

    @pl.when(s == 0)
    def _():
        acc_ref[...] = jnp.zeros_like(acc_ref)
        z = stage_a(0, 0)
        stage_a(0, 1)
        stage_b(0, 0, 0, z)
        z = stage_a(1, 0)
        stage_b(0, 0, 1, z)
        z = stage_a(1, 1)
        stage_b(0, 0, 2, z)
        stage_b(0, 0, 3, z)

    @pl.when(jnp.logical_and(s > 0, s < last))
    def _():
        ka, kb = 2 * s - 1, 2 * s
        z = stage_a(0, 0)
        stage_b(1, ka, 0, zero0)
        z2 = stage_a(0, 1)
        stage_b(1, ka, 1, z)
        z = stage_c(0, 0)
        stage_b(1, ka, 2, z2)
        z2 = stage_c(0, 1)
        stage_b(1, ka, 3, z)
        z = stage_a(1, 0)
        stage_b(0, kb, 0, z2)
        z2 = stage_a(1, 1)
        stage_b(0, kb, 1, z)
        z = stage_c(1, 0)
        stage_b(0, kb, 2, z2)
        stage_c(1, 1)
        stage_b(0, kb, 3, z)

    @pl.when(s == last)
    def _():
        ka = 2 * s - 1
        z = stage_c(0, 0)
        stage_b(1, ka, 0, zero0)
        stage_b(1, ka, 1, z)
        z = stage_c(0, 1)
        stage_b(1, ka, 2, z)
        stage_b(1, ka, 3, z)
        stage_c(1, 0)
        stage_c(1, 1)
        y = _rms(x1_ref[...] + acc_ref[...].T, g_ref[...])
        tile = pl.program_id(0)

        @pl.when(tile < tiles_p)
        def _():
            yp_ref[...] = y

        @pl.when(tile >= tiles_p)
        def _():
            ys_ref[...] = y


def _peer(xn, u_tab, vt_tab, key2, thr, p1, p2, x1, g, t_p):
    t, d = xn.shape
    n_exp = u_tab.shape[0]
    tt, eb = TOKEN_TILE, EXPERT_BLOCK
    lane_tiles = tt // LANES
    n_pairs = n_exp // (2 * eb)
    tab_spec = pl.BlockSpec((PEER_HEADS, lane_tiles, N_KEYS, LANES), lambda i, s: (0, i, 0, 0))
    tiles_p = t_p // tt
    return pl.pallas_call(
        functools.partial(_peer_kernel, tiles_p=tiles_p),
        grid=(t // tt, n_pairs + 1),
        in_specs=[pl.BlockSpec((tt, d), lambda i, s: (i, 0)),
                  pl.BlockSpec((2 * eb, d), lambda i, s: (jnp.minimum(s, n_pairs - 1), 0)),
                  pl.BlockSpec((d, 2 * eb), lambda i, s: (0, jnp.maximum(s - 1, 0))),
                  tab_spec, tab_spec, tab_spec, tab_spec,
                  pl.BlockSpec((tt, d), lambda i, s: (i, 0), pipeline_mode=pl.Buffered(1)),
                  pl.BlockSpec((1, d), lambda i, s: (0, 0))],
        out_specs=_stream_specs(tt, d, tiles_p, pipeline_mode=pl.Buffered(1)),
        out_shape=[jax.ShapeDtypeStruct((t_p, d), F32), jax.ShapeDtypeStruct((t - t_p, d), F32)],
        scratch_shapes=[pltpu.VMEM((d, tt), F32), pltpu.VMEM((eb, tt), F32), pltpu.VMEM((eb, tt), F32),
                        pltpu.VMEM((eb, tt), BF16), pltpu.VMEM((eb, tt), BF16)],
        compiler_params=pltpu.CompilerParams(dimension_semantics=("arbitrary", "arbitrary"),
                                             vmem_limit_bytes=VMEM_LIMIT),
        name="peer",
    )(xn, u_tab, vt_tab, key2, thr, p1, p2, x1, g)


def _ssm_params(lam_re, lam_im, log_step, b_re, b_im, c_re, c_im, d_skip):
    g, p = lam_re.shape
    hh = b_re.shape[2]
    step = jnp.exp(log_step)[:, None]
    decay = jnp.exp(lam_re * step)
    ab_re = decay * jnp.cos(lam_im * step)
    ab_im = decay * jnp.sin(lam_im * step)
    inv = 1.0 / (lam_re * lam_re + lam_im * lam_im)
    k_re = ((ab_re - 1.0) * lam_re + ab_im * lam_im) * inv
    k_im = (ab_im * lam_re - (ab_re - 1.0) * lam_im) * inv
    bb_re = k_re[..., None] * b_re - k_im[..., None] * b_im
    bb_im = k_re[..., None] * b_im + k_im[..., None] * b_re
    eye = jnp.eye(g, dtype=F32)

    def in_mat(bb):
        return jnp.einsum("gph,gk->ghkp", bb, eye).reshape(g * hh, g * p)

    def out_mat(cc):
        return jnp.einsum("ghp,gk->gpkh", cc, eye).reshape(g * p, g * hh)

    wb = jnp.concatenate([in_mat(bb_re), in_mat(bb_im)], axis=1).astype(BF16)
    wc = jnp.concatenate([out_mat(c_re), -out_mat(c_im)], axis=0).astype(BF16)

    a_re = ab_re.reshape(1, g * p)
    a_im = ab_im.reshape(1, g * p)
    pows = [(a_re, a_im)]
    for _ in range(SUBLANES - 1):
        q_re, q_im = pows[-1]
        pows.append((q_re * a_re - q_im * a_im, q_re * a_im + q_im * a_re))
    row = jnp.arange(SUBLANES)[:, None]
    lvl = jnp.stack([jnp.stack([jnp.where(row >= k, pows[k - 1][0], 0.0),
                                jnp.where(row >= k, pows[k - 1][1], 0.0)]) for k in (1, 2, 4)])
    pw = jnp.stack([jnp.concatenate([q[0] for q in pows], axis=0),
                    jnp.concatenate([q[1] for q in pows], axis=0)])
    return wb, wc, d_skip.reshape(1, g * hh), lvl, pw


def kernel(x_prompt, x_sample, state_ssm_re, state_ssm_im, cache_conv, norm_mix, w_in, b_gate,
           ssm_lambda_re, ssm_lambda_im, ssm_log_step, ssm_b_re, ssm_b_im, ssm_c_re, ssm_c_im,
           ssm_d, w_ssm_glu, conv_w, conv_b, conv_ln_g, conv_ln_b, w_conv_out, w_out, norm_ffn,
           peer_w_q, peer_keys1, peer_keys2, peer_u, peer_v, norm_final):
    depth = w_in.shape[0]
    assert depth == 1, "single-layer stack only"
    n_p, l_p, d = x_prompt.shape
    n_s, l_s, _ = x_sample.shape
    assert l_p % CHUNK == 0 and l_s % CHUNK == 0
    t_p, t_s = n_p * l_p, n_s * l_s
    t = t_p + t_s
    assert t_p % TOKEN_TILE == 0 and t_s % TOKEN_TILE == 0
    g, p = ssm_lambda_re.shape[1:]
    n_state = g * p
    d_ssm = g * ssm_b_re.shape[3]
    d_conv = conv_w.shape[2]
    hist_len = CONV_WIDTH - 1

    xp = x_prompt.reshape(t_p, d)
    xs = x_sample.reshape(t_s, d)

    cp, cs = l_p // CHUNK, l_s // CHUNK
    seq_p = jnp.full((n_p, cp), -1, jnp.int32).at[:, 0].set(0).reshape(-1)
    seq_s = jnp.full((n_s, cs), -1, jnp.int32).at[:, 0].set(1 + jnp.arange(n_s, dtype=jnp.int32)).reshape(-1)
    seq = jnp.concatenate([seq_p, seq_s])
    h0 = jnp.concatenate([state_ssm_re[0].reshape(n_s, n_state), state_ssm_im[0].reshape(n_s, n_state)], axis=1)
    h0 = jnp.pad(h0, ((1, 0), (0, 0)))
    hist = jnp.pad(cache_conv[0], ((1, 0), (HIST_ROWS - hist_len, 0), (0, 0)))

    wb, wc, d_skip, lvl, pw = _ssm_params(ssm_lambda_re[0], ssm_lambda_im[0], ssm_log_step[0],
                                          ssm_b_re[0], ssm_b_im[0], ssm_c_re[0], ssm_c_im[0], ssm_d[0])

    u, v, gate = _in_proj(xp, xs, norm_mix, w_in[0].astype(BF16), b_gate, d_ssm, d_conv)
    branch_a, hend = _ssm(seq, u, h0, wb, wc, d_skip, w_ssm_glu[0].astype(BF16), lvl, pw, d)
    cw = jnp.pad(conv_w[0], ((0, HIST_ROWS - CONV_WIDTH), (0, 0)))
    x1 = _mix(seq, v, hist, cw, conv_b, conv_ln_g, conv_ln_b, w_conv_out[0].astype(BF16),
              gate, branch_a, xp, xs, w_out[0].astype(BF16))
    xn2, key2, thr, p1, p2 = _query(x1, norm_ffn, peer_w_q[0].astype(BF16),
                                    peer_keys1[0].astype(BF16), peer_keys2[0].astype(BF16))
    y_p, y_s = _peer(xn2, peer_u[0].astype(BF16), peer_v[0].T.astype(BF16), key2, thr, p1, p2, x1,
                     norm_final.reshape(1, d), t_p)

    y_prompt = y_p.reshape(n_p, l_p, d)
    y_sample = y_s.reshape(n_s, l_s, d)
    last_p = hend[:n_p * cp].reshape(n_p, cp, 2 * n_state)[:, -1]
    last_s = hend[n_p * cp:].reshape(n_s, cs, 2 * n_state)[:, -1]
    v_p = v[:t_p].reshape(n_p, l_p, d_conv)[:, l_p - hist_len:]
    v_s = v[t_p:].reshape(n_s, l_s, d_conv)[:, l_s - hist_len:]
    return (y_prompt, y_sample,
            last_p[:, :n_state].reshape(1, n_p, g, p), last_p[:, n_state:].reshape(1, n_p, g, p), v_p[None],
            last_s[:, :n_state].reshape(1, n_s, g, p), last_s[:, n_state:].reshape(1, n_s, g, p), v_s[None])
```

```python
import functools
import math

import jax
import jax.numpy as jnp
from jax import lax
from jax.experimental import pallas as pl
from jax.experimental.pallas import tpu as pltpu

F32 = jnp.float32
BF16 = jnp.bfloat16

EPS = 1e-6
CHUNK = 64
SSM_GROUP = 16
SSM_STATE = 64
CONV_WIDTH = 31
HIST_ROWS = 32
N_KEYS = 128
PEER_HEADS = 8
PEER_TOPK = 16
LANES = 128
SUBLANES = 8
TOKEN_TILE = 512
EXPERT_BLOCK = 1024
JROWS = 64
VMEM_LIMIT = 58 * 1024 * 1024
NEG_INF = float("-inf")


def _rms(x, g):
    ms = jnp.mean(x * x, axis=-1, keepdims=True)
    return (x * lax.rsqrt(ms + EPS)) * g


def _gelu(x):
    return 0.5 * x * (1.0 + lax.erf(x * (1.0 / math.sqrt(2.0))))


def _sigmoid(x):
    return 1.0 / (1.0 + jnp.exp(-x))


def _const_spec(shape):
    nd = len(shape)
    return pl.BlockSpec(shape, lambda *_: (0,) * nd)


def _stream_specs(tm, d, tiles_p, pipeline_mode=None):
    return [pl.BlockSpec((tm, d), lambda i, *_: (jnp.minimum(i, tiles_p - 1), 0), pipeline_mode=pipeline_mode),
            pl.BlockSpec((tm, d), lambda i, *_: (jnp.maximum(i - tiles_p, 0), 0), pipeline_mode=pipeline_mode)]


def _in_proj_kernel(xp_ref, xs_ref, g_ref, w_ref, bg_ref, u_ref, v_ref, gate_ref, *, d_ssm, d_conv, tiles_p):
    x = jnp.where(pl.program_id(0) < tiles_p, xp_ref[...], xs_ref[...])
    xn = _rms(x, g_ref[...]).astype(BF16)
    u_ref[...] = jnp.dot(xn, w_ref[:, :d_ssm], preferred_element_type=F32)
    pq = jnp.dot(xn, w_ref[:, d_ssm:d_ssm + 2 * d_conv], preferred_element_type=F32)
    v_ref[...] = pq[:, :d_conv] * _sigmoid(pq[:, d_conv:])
    gl = jnp.dot(xn, w_ref[:, d_ssm + 2 * d_conv:], preferred_element_type=F32)
    gate_ref[...] = _sigmoid(gl + bg_ref[...]).astype(BF16)


def _in_proj(xp, xs, g, w_in, b_gate, d_ssm, d_conv):
    d = xp.shape[1]
    t = xp.shape[0] + xs.shape[0]
    d_in = w_in.shape[1]
    n_gate = d_in - d_ssm - 2 * d_conv
    tm = TOKEN_TILE
    tiles_p = xp.shape[0] // tm
    return pl.pallas_call(
        functools.partial(_in_proj_kernel, d_ssm=d_ssm, d_conv=d_conv, tiles_p=tiles_p),
        grid=(t // tm,),
        in_specs=_stream_specs(tm, d, tiles_p) +
                 [_const_spec((1, d)), _const_spec((d, d_in)), _const_spec((1, n_gate))],
        out_specs=[pl.BlockSpec((tm, d_ssm), lambda i: (i, 0)),
                   pl.BlockSpec((tm, d_conv), lambda i: (i, 0)),
                   pl.BlockSpec((tm, n_gate), lambda i: (i, 0))],
        out_shape=[jax.ShapeDtypeStruct((t, d_ssm), F32),
                   jax.ShapeDtypeStruct((t, d_conv), F32),
                   jax.ShapeDtypeStruct((t, n_gate), BF16)],
        compiler_params=pltpu.CompilerParams(dimension_semantics=("arbitrary",),
                                             vmem_limit_bytes=VMEM_LIMIT),
        name="in_proj",
    )(xp, xs, g, w_in, b_gate)


def _ssm_kernel(seq_ref, u_ref, h0_ref, wb_ref, wc_ref, d_ref, wglu_ref, lvl_ref, pw_ref,
                a_ref, hend_ref, bu_ref, cre_ref, cim_ref, *, n_state, d_model):
    tile = u_ref.shape[0]
    chunks = tile // CHUNK
    step = pl.program_id(0)

    @pl.when(step == 0)
    def _():
        cre_ref[...] = jnp.zeros_like(cre_ref)
        cim_ref[...] = jnp.zeros_like(cim_ref)

    u = u_ref[...]
    ub = u.astype(BF16)
    dh = u.shape[1] // 2
    nh = n_state // 2
    for part in (0, n_state):
        for hf in range(2):
            cols = slice(part + hf * nh, part + (hf + 1) * nh)
            bu_ref[:, cols] = jnp.dot(ub[:, hf * dh:(hf + 1) * dh], wb_ref[hf * dh:(hf + 1) * dh, cols],
                                      preferred_element_type=F32)

    for j in range(chunks):
        seq = seq_ref[step * chunks + j]
        start = seq >= 0
        row = jnp.maximum(seq, 0)
        c_re = jnp.where(start, h0_ref[pl.ds(row, 1), :n_state], cre_ref[...])
        c_im = jnp.where(start, h0_ref[pl.ds(row, 1), n_state:], cim_ref[...])

        def block(b, carry, j=j):
            c_re, c_im = carry
            r0 = pl.multiple_of((j * (CHUNK // SUBLANES) + b) * SUBLANES, SUBLANES)
            x_re = bu_ref[pl.ds(r0, SUBLANES), :n_state]
            x_im = bu_ref[pl.ds(r0, SUBLANES), n_state:]
            for lvl, k in enumerate((1, 2, 4)):
                m_re = lvl_ref[lvl, 0]
                m_im = lvl_ref[lvl, 1]
                t_re = pltpu.roll(x_re, k, 0)
                t_im = pltpu.roll(x_im, k, 0)
                x_re, x_im = (x_re + m_re * t_re - m_im * t_im,
                              x_im + m_re * t_im + m_im * t_re)
            p_re = pw_ref[0]
            p_im = pw_ref[1]
            h_re = x_re + p_re * c_re - p_im * c_im
            h_im = x_im + p_re * c_im + p_im * c_re
            bu_ref[pl.ds(r0, SUBLANES), :n_state] = h_re
            bu_ref[pl.ds(r0, SUBLANES), n_state:] = h_im
            return h_re[SUBLANES - 1:], h_im[SUBLANES - 1:]

        c_re, c_im = lax.fori_loop(0, CHUNK // SUBLANES, block, (c_re, c_im))
        cre_ref[...] = c_re
        cim_ref[...] = c_im
        hend_ref[j:j + 1, :n_state] = c_re
        hend_ref[j:j + 1, n_state:] = c_im

    ys = []
    for hf in range(2):
        out = slice(hf * dh, (hf + 1) * dh)
        re = slice(hf * nh, (hf + 1) * nh)
        im = slice(n_state + hf * nh, n_state + (hf + 1) * nh)
        ys.append(jnp.dot(bu_ref[:, re].astype(BF16), wc_ref[re, out], preferred_element_type=F32)
                  + jnp.dot(bu_ref[:, im].astype(BF16), wc_ref[im, out], preferred_element_type=F32))
    y = jnp.concatenate(ys, axis=1) + d_ref[...] * u
    pq =jnp.dot(_gelu(y).astype(BF16), wglu_ref[...], preferred_element_type=F32)
    a_ref[...] = (pq[:, :d_model] * _sigmoid(pq[:, d_model:])).astype(BF16)


def _ssm(seq, u, h0, wb, wc, d_skip, wglu, lvl, pw, d_model):
    t, d_ssm = u.shape
    n_state = wb.shape[1] // 2
    tm = TOKEN_TILE
    chunks = tm // CHUNK
    grid_spec = pltpu.PrefetchScalarGridSpec(
        num_scalar_prefetch=1,
        grid=(t // tm,),
        in_specs=[pl.BlockSpec((tm, d_ssm), lambda i, f: (i, 0)),
                  _const_spec(h0.shape),
                  _const_spec(wb.shape), _const_spec(wc.shape), _const_spec(d_skip.shape),
                  _const_spec(wglu.shape), _const_spec(lvl.shape), _const_spec(pw.shape)],
        out_specs=[pl.BlockSpec((tm, d_model), lambda i, f: (i, 0)),
                   pl.BlockSpec((chunks, 2 * n_state), lambda i, f: (i, 0))],
        scratch_shapes=[pltpu.VMEM((tm, 2 * n_state), F32),
                        pltpu.VMEM((1, n_state), F32), pltpu.VMEM((1, n_state), F32)],
    )
    return pl.pallas_call(
        functools.partial(_ssm_kernel, n_state=n_state, d_model=d_model),
        grid_spec=grid_spec,
        out_shape=[jax.ShapeDtypeStruct((t, d_model), BF16),
                   jax.ShapeDtypeStruct((t // CHUNK, 2 * n_state), F32)],
        compiler_params=pltpu.CompilerParams(dimension_semantics=("arbitrary",),
                                             vmem_limit_bytes=VMEM_LIMIT),
        name="ssm",
    )(seq, u, h0, wb, wc, d_skip, wglu, lvl, pw)


def _mix_kernel(seq_ref, v_ref, hist_ref, cw_ref, cb_ref, lg_ref, lb_ref, wco_ref,
                gate_ref, a_ref, xp_ref, xs_ref, wout_ref, x1_ref, win_ref, shift_ref, act_ref,
                *, d_model, tiles_p):
    tile = v_ref.shape[0]
    chunks = tile // CHUNK
    step = pl.program_id(0)
    lead = HIST_ROWS - (CONV_WIDTH - 1)

    @pl.when(step == 0)
    def _():
        win_ref[...] = jnp.zeros_like(win_ref)

    def chunk_body(j, carry):
        r0 = pl.multiple_of(j * CHUNK, CHUNK)
        seq = seq_ref[step * chunks + j]
        win_ref[:HIST_ROWS] = jnp.where(seq >= 0, hist_ref[jnp.maximum(seq, 0)], win_ref[CHUNK:])
        win_ref[HIST_ROWS:] = v_ref[pl.ds(r0, CHUNK), :]
        span = HIST_ROWS + CHUNK - SUBLANES
        for r in range(1, SUBLANES):
            shift_ref[r - 1] = win_ref[r:r + span, :]
        acc = None
        for k in range(CONV_WIDTH):
            q, r = divmod(lead + k, SUBLANES)
            rows = win_ref[q * SUBLANES:q * SUBLANES + CHUNK, :] if r == 0 else \
                shift_ref[r - 1, q * SUBLANES:q * SUBLANES + CHUNK, :]
            term = cw_ref[k:k + 1, :] * rows
            acc = term if acc is None else acc + term
        acc = acc + cb_ref[...]
        mu = jnp.mean(acc, axis=-1, keepdims=True)
        xc = acc - mu
        ln = (xc * lax.rsqrt(jnp.mean(xc * xc, axis=-1, keepdims=True) + EPS)) * lg_ref[...] + lb_ref[...]
        act_ref[pl.ds(r0, CHUNK), :] = (ln * _sigmoid(ln)).astype(BF16)
        return carry

    lax.fori_loop(0, chunks, chunk_body, 0)
    branch_b = jnp.dot(act_ref[...], wco_ref[...], preferred_element_type=F32)
    ga = gate_ref[:, :d_model].astype(F32)
    gb = gate_ref[:, d_model:].astype(F32)
    m = ga * a_ref[...].astype(F32) + gb * branch_b
    x = jnp.where(step < tiles_p, xp_ref[...], xs_ref[...])
    x1_ref[...] = x + jnp.dot(m.astype(BF16), wout_ref[...], preferred_element_type=F32)


def _mix(seq, v, hist, conv_w, conv_b, ln_g, ln_b, w_conv_out, gate, branch_a, xp, xs, w_out):
    t, d_conv = v.shape
    d_model = xp.shape[1]
    tm = TOKEN_TILE
    tiles_p = xp.shape[0] // tm
    grid_spec = pltpu.PrefetchScalarGridSpec(
        num_scalar_prefetch=1,
        grid=(t // tm,),
        in_specs=[pl.BlockSpec((tm, d_conv), lambda i, f: (i, 0)),
                  _const_spec(hist.shape),
                  _const_spec(conv_w.shape), _const_spec(conv_b.shape), _const_spec(ln_g.shape),
                  _const_spec(ln_b.shape), _const_spec(w_conv_out.shape),
                  pl.BlockSpec((tm, 2 * d_model), lambda i, f: (i, 0)),
                  pl.BlockSpec((tm, d_model), lambda i, f: (i, 0))] +
                 _stream_specs(tm, d_model, tiles_p) + [_const_spec(w_out.shape)],
        out_specs=pl.BlockSpec((tm, d_model), lambda i, f: (i, 0)),
        scratch_shapes=[pltpu.VMEM((HIST_ROWS + CHUNK, d_conv), F32),
                        pltpu.VMEM((SUBLANES - 1, HIST_ROWS + CHUNK - SUBLANES, d_conv), F32),
                        pltpu.VMEM((tm, d_conv), BF16)],
    )
    return pl.pallas_call(
        functools.partial(_mix_kernel, d_model=d_model, tiles_p=tiles_p),
        grid_spec=grid_spec,
        out_shape=jax.ShapeDtypeStruct((t, d_model), F32),
        compiler_params=pltpu.CompilerParams(dimension_semantics=("arbitrary",),
                                             vmem_limit_bytes=VMEM_LIMIT),
        name="mix",
    )(seq, v, hist, conv_w, conv_b, ln_g, ln_b, w_conv_out, gate, branch_a, xp, xs, w_out)


def _top16(s, iota):
    iota16 = iota[:PEER_TOPK]

    def rnd(r, carry):
        work, rank, vals = carry
        rf = jnp.asarray(r, F32)
        m = jnp.max(work, axis=0, keepdims=True)
        first = jnp.min(jnp.where(work == m, iota, float(N_KEYS)), axis=0, keepdims=True)
        sel = iota == first
        rank = jnp.where(sel, rf, rank)
        work = jnp.where(sel, NEG_INF, work)
        vals = jnp.where(iota16 == rf, m, vals)
        return work, rank, vals

    init = (s, jnp.full(s.shape, float(PEER_TOPK), F32), jnp.zeros((PEER_TOPK, s.shape[1]), F32))
    _, rank, vals = lax.fori_loop(0, PEER_TOPK, rnd, init)
    return vals, rank


def _merge16(v1, v2, iota16):
    m0 = v1[0:1] + v2[0:1]

    def rnd(_, carry):
        n, front, z = carry
        m = jnp.max(front, axis=0, keepdims=True)
        a_sel = jnp.min(jnp.where(front == m, iota16, float(PEER_TOPK)), axis=0, keepdims=True)
        sel = iota16 == a_sel
        n = jnp.where(sel, n + 1.0, n)
        k = jnp.max(jnp.where(sel, n, 0.0), axis=0, keepdims=True)
        v2k = jnp.max(jnp.where(iota16 == k, v2, NEG_INF), axis=0, keepdims=True)
        front = jnp.where(sel, v1 + v2k, front)
        return n, front, z + jnp.exp(m - m0)

    init = (jnp.zeros_like(v1), v1 + v2[0:1], jnp.zeros_like(m0))
    n, _, z = lax.fori_loop(0, PEER_TOPK, rnd, init)
    return n, z


def _batcher_network(n):
    def merge(lo, hi, r):
        step = r * 2
        if step < hi - lo:
            yield from merge(lo, hi, step)
            yield from merge(lo + r, hi, step)
            yield from ((i, i + r) for i in range(lo + r, hi - r, step))
        else:
            yield (lo, lo + r)

    def sort(lo, hi):
        if hi - lo >= 1:
            mid = lo + (hi - lo) // 2
            yield from sort(lo, mid)
            yield from sort(mid + 1, hi)
            yield from merge(lo, hi, 1)

    return tuple(sort(0, n - 1))


_SORT16 = _batcher_network(PEER_TOPK)


def _cmpx(x, i, j):
    if x[j] is None:
        return
    if x[i] is None:
        x[i], x[j] = x[j], None
        return
    x[i], x[j] = jnp.maximum(x[i], x[j]), jnp.minimum(x[i], x[j])


def _top16_of_sublane_lists(x):
    n = PEER_TOPK
    for shift in (4, 2, 1):
        y = [None if v is None else pltpu.roll(v, shift, 0) for v in x]
        t = []
        for r in range(n):
            a, b = x[r], y[n - 1 - r]
            t.append(b if a is None else a if b is None else jnp.maximum(a, b))
        for d in (8, 4, 2, 1):
            for r in range(n):
                if r & d == 0:
                    _cmpx(t, r, r | d)
        x = t
    return x


def _sorted_top16(s):
    x = [s[v * SUBLANES:(v + 1) * SUBLANES] for v in range(s.shape[0] // SUBLANES)]
    for i, j in _SORT16:
        _cmpx(x, i, j)
    return _top16_of_sublane_lists(x)


def _count_ge(tiles, bound):
    cnt = jnp.where(tiles[0] >= bound, 1.0, 0.0)
    for v in tiles[1:]:
        cnt = cnt + jnp.where(v >= bound, 1.0, 0.0)
    return jnp.sum(cnt, axis=0, keepdims=True)


def _route_by_value(s1, s2):
    v1 = _sorted_top16(s1)
    v2 = _sorted_top16(s2)
    sub = lax.broadcasted_iota(jnp.int32, (SUBLANES, LANES), 0)

    def by_sublane(vs):
        out = vs[0]
        for r in range(1, SUBLANES):
            out = jnp.where(sub == r, vs[r], out)
        return out

    w1lo, w1hi = by_sublane(v1[:SUBLANES]), by_sublane(v1[SUBLANES:])
    w2lo, w2hi = by_sublane(v2[:SUBLANES]), by_sublane(v2[SUBLANES:])
    cands = [v1[0] + w2lo, v1[0] + w2hi, v1[1] + w2lo,
             jnp.where(sub >= 2, w1lo + v2[0], NEG_INF), w1hi + v2[0],
             jnp.where(sub >= 2, w1lo + v2[1], NEG_INF),
             jnp.where((sub >= 2) & (sub <= 4), v1[2] + w2lo, NEG_INF),
             jnp.where((sub >= 2) & (sub <= 3), v1[3] + w2lo, NEG_INF),
             jnp.where(sub == 2, v1[4] + w2lo, NEG_INF)]
    x = list(cands) + [None] * (PEER_TOPK - len(cands))
    for i, j in _SORT16:
        _cmpx(x, i, j)
    top = _top16_of_sublane_lists(x)
    tau = top[PEER_TOPK - 1]
    z = jnp.exp(top[1] - top[0]) + 1.0
    for r in range(2, PEER_TOPK):
        z = z + jnp.exp(top[r] - top[0])

    s1_tiles = [s1[v * SUBLANES:(v + 1) * SUBLANES] for v in range(N_KEYS // SUBLANES)]
    s2_tiles = [s2[v * SUBLANES:(v + 1) * SUBLANES] for v in range(N_KEYS // SUBLANES)]
    k = float(PEER_TOPK)
    tied = ((_count_ge(s1_tiles, v1[PEER_TOPK - 1]) != k) | (_count_ge(s2_tiles, v2[PEER_TOPK - 1]) != k)
            | (_count_ge(cands, tau) != k))

    n = []
    for a in range(PEER_TOPK):
        na = jnp.where(v1[a] + v2[0] >= tau, 1.0, 0.0)
        for b in range(1, PEER_TOPK // (a + 1)):
            na = na + jnp.where(v1[a] + v2[b] >= tau, 1.0, 0.0)
        n.append(na)

    def pick(vals, bits):
        if len(vals) == 1:
            return vals[0]
        mid = len(vals) // 2
        return jnp.where(bits[0], pick(vals[mid:], bits[1:]), pick(vals[:mid], bits[1:]))

    def rank_bits(x, v):
        b8 = v[7] > x
        b4 = pick([v[3], v[11]], [b8]) > x
        b2 = pick([v[1], v[5], v[9], v[13]], [b8, b4]) > x
        b1 = pick(v[0:PEER_TOPK:2], [b8, b4, b2]) > x
        return [b8, b4, b2, b1], v[PEER_TOPK - 1] > x

    key2_tiles, thr_tiles = [], []
    for x in s2_tiles:
        bits, beyond = rank_bits(x, v2)
        rank = (jnp.where(bits[0], 8.0, 0.0) + jnp.where(bits[1], 4.0, 0.0)
                + jnp.where(bits[2], 2.0, 0.0) + jnp.where(bits[3], 1.0, 0.0))
        key2_tiles.append(jnp.where(beyond, 0.0, k - rank))
    for x in s1_tiles:
        bits, beyond = rank_bits(x, v1)
        thr_tiles.append((k + 1.0) - jnp.where(beyond, 0.0, pick(n, bits)))
    key2 = jnp.concatenate(key2_tiles, axis=0)
    thr = jnp.concatenate(thr_tiles, axis=0)
    p1 =jnp.exp(s1 - v1[0][0:1]) * (1.0 / z[0:1])
    p2 = jnp.exp(s2 - v2[0][0:1])
    return key2, thr, p1, p2, jnp.where(tied, 1.0, 0.0)


def _query_kernel(x1_ref, g_ref, wq_ref, k1_ref, k2_ref,
                  xn_ref, key2_ref, thr_ref, p1_ref, p2_ref, s1_ref, s2_ref, tied_ref):
    tile = x1_ref.shape[0]
    lane_tiles = tile // LANES
    half = k1_ref.shape[1]
    xn = _rms(x1_ref[...], g_ref[...]).astype(BF16)
    xn_ref[...] = xn
    q = jnp.dot(xn, wq_ref[...], preferred_element_type=F32).astype(BF16)
    nt = (((1,), (1,)), ((), ()))
    for h in range(PEER_HEADS):
        q1 = q[:, 2 * h * half:(2 * h + 1) * half]
        q2 = q[:, (2 * h + 1) * half:(2 * h + 2) * half]
        s1 = lax.dot_general(k1_ref[...], q1, nt, preferred_element_type=F32)
        s2 = lax.dot_general(k2_ref[...], q2, nt, preferred_element_type=F32)
        for c in range(lane_tiles):
            s1_ref[h, c] = s1[:, c * LANES:(c + 1) * LANES]
            s2_ref[h, c] = s2[:, c * LANES:(c + 1) * LANES]

    tied_ref[...] = jnp.zeros_like(tied_ref)

    def route_by_value(idx, carry):
        h = idx // lane_tiles
        c = idx % lane_tiles
        key2, thr, p1, p2, tied = _route_by_value(s1_ref[h, c], s2_ref[h, c])
        key2_ref[h, c] = key2
        thr_ref[h, c] = thr
        p1_ref[h, c] = p1
        p2_ref[h, c] = p2
        tied_ref[...] = jnp.maximum(tied_ref[...], tied)
        return carry

    lax.fori_loop(0, PEER_HEADS * lane_tiles, route_by_value, 0)

    def route_by_rank(idx, carry):
        h = idx // lane_tiles
        c = idx % lane_tiles
        iota = lax.broadcasted_iota(jnp.int32, (N_KEYS, LANES), 0).astype(F32)
        s1 = s1_ref[h, c]
        s2 = s2_ref[h, c]
        v1, rank1 = _top16(s1, iota)
        v2, rank2 = _top16(s2, iota)
        n, z = _merge16(v1, v2, iota[:PEER_TOPK])
        cnt = jnp.zeros_like(rank1)
        for a in range(PEER_TOPK):
            cnt = jnp.where(rank1 == float(a), n[a:a + 1], cnt)
        key2_ref[h, c] = float(PEER_TOPK) - rank2
        thr_ref[h, c] = float(PEER_TOPK + 1) - cnt
        p1_ref[h, c] = jnp.exp(s1 - v1[0:1]) / z
        p2_ref[h, c] = jnp.exp(s2 - v2[0:1])
        return carry

    @pl.when(jnp.max(tied_ref[...]) > 0.0)
    def _():
        lax.fori_loop(0, PEER_HEADS * lane_tiles, route_by_rank, 0)


def _query(x1, g, w_q, keys1, keys2):
    t, d = x1.shape
    tm = TOKEN_TILE
    lane_tiles = tm // LANES
    tab_block = (PEER_HEADS, lane_tiles, N_KEYS, LANES)
    tab_spec = pl.BlockSpec(tab_block, lambda i: (0, i, 0, 0))
    tab_shape = jax.ShapeDtypeStruct((PEER_HEADS, t // LANES, N_KEYS, LANES), F32)
    return pl.pallas_call(
        _query_kernel,
        grid=(t // tm,),
        in_specs=[pl.BlockSpec((tm, d), lambda i: (i, 0)), _const_spec(g.shape),
                  _const_spec(w_q.shape), _const_spec(keys1.shape), _const_spec(keys2.shape)],
        out_specs=[pl.BlockSpec((tm, d), lambda i: (i, 0)), tab_spec, tab_spec, tab_spec, tab_spec],
        out_shape=[jax.ShapeDtypeStruct((t, d), BF16), tab_shape, tab_shape, tab_shape, tab_shape],
        scratch_shapes=[pltpu.VMEM(tab_block, F32), pltpu.VMEM(tab_block, F32),
                        pltpu.VMEM((1, LANES), F32)],
        compiler_params=pltpu.CompilerParams(dimension_semantics=("arbitrary",),
                                             vmem_limit_bytes=VMEM_LIMIT),
        name="query",
    )(x1, g, w_q, keys1, keys2)


def _peer_kernel(xn_ref, u_ref, vt_ref, key2_ref, thr_ref, p1_ref, p2_ref, x1_ref, g_ref,
                 yp_ref, ys_ref, acc_ref, act0_ref, act1_ref, coef0_ref, coef1_ref, *, tiles_p):
    s = pl.program_id(1)
    last = pl.num_programs(1) - 1
    eb = u_ref.shape[0] // 2
    tt = xn_ref.shape[0]
    half = tt // 2
    rows = eb // N_KEYS
    nt = (((1,), (1,)), ((), ()))
    act_refs = (act0_ref, act1_ref)
    coef_refs = (coef0_ref, coef1_ref)

    def tail_zero(r):
        return jnp.minimum(jnp.abs(r[r.shape[0] - JROWS:, r.shape[1] - LANES:]), 0.0)

    def stage_a(slot, n):
        r = lax.dot_general(u_ref[slot * eb:(slot + 1) * eb, :], xn_ref[n * half:(n + 1) * half, :], nt,
                            preferred_element_type=F32)
        act_refs[slot][:, n * half:(n + 1) * half] = r
        return tail_zero(r)

    def stage_b(slot, k, c, zero):
        act_ref, coef_ref = act_refs[slot], coef_refs[slot]
        zero_bf = zero.astype(BF16)
        for j0 in range(0, N_KEYS, JROWS):
            ws = [zero_bf] * rows
            for h in range(PEER_HEADS):
                k2 = key2_ref[h, c, j0:j0 + JROWS, :].astype(BF16)
                p2 = p2_ref[h, c, j0:j0 + JROWS, :].astype(BF16)
                for il in range(rows):
                    i = k * rows + il
                    thr = jnp.broadcast_to(thr_ref[h, c, pl.ds(i, 1), :], (JROWS, LANES)).astype(BF16)
                    p1 = jnp.broadcast_to(p1_ref[h, c, pl.ds(i, 1), :], (JROWS, LANES)).astype(BF16)
                    ws[il] = ws[il] + jnp.where(k2 >= thr, p2, jnp.zeros_like(p2)) * p1
            for il in range(rows):
                blk = (slice(il * N_KEYS + j0, il * N_KEYS + j0 + JROWS), slice(c * LANES, (c + 1) * LANES))
                coef_ref[blk] = ws[il] * _gelu(act_ref[blk]).astype(BF16)

    def stage_c(slot, n):
        cols = slice(n * half, (n + 1) * half)
        r = acc_ref[:, cols] + jnp.dot(vt_ref[:, slot * eb:(slot + 1) * eb], coef_refs[slot][:, cols],
                                       preferred_element_type=F32)
        acc_ref[:, cols] = r
        return tail_zero(r)

    zero0 = jnp.zeros((JROWS, LANES), F32)

    @pl.when(s == 0)
    def _():
        acc_ref[...] = jnp.zeros_like(acc_ref)
        z = stage_a(0, 0)
        stage_a(0, 1)
        stage_b(0, 0, 0, z)
        z = stage_a(1, 0)
        stage_b(0, 0, 1, z)
        z = stage_a(1, 1)
        stage_b(0, 0, 2, z)
        stage_b(0, 0, 3, z)

    @pl.when(jnp.logical_and(s > 0, s < last))
    def _():
        ka, kb = 2 * s - 1, 2 * s
        z = stage_a(0, 0)
        stage_b(1, ka, 0, zero0)
        z2 = stage_a(0, 1)
        stage_b(1, ka, 1, z)
        z = stage_c(0, 0)
        stage_b(1, ka, 2, z2)
        z2 = stage_c(0, 1)
        stage_b(1, ka, 3, z)
        z = stage_a(1, 0)
        stage_b(0, kb, 0, z2)
        z2 = stage_a(1, 1)
        stage_b(0, kb, 1, z)
        z = stage_c(1, 0)
        stage_b(0, kb, 2, z2)
        stage_c(1, 1)
        stage_b(0, kb, 3, z)

    @pl.when(s == last)
    def _():
        ka = 2 * s - 1
        z = stage_c(0, 0)
        stage_b(1, ka, 0, zero0)
        stage_b(1, ka, 1, z)
        z = stage_c(0, 1)
        stage_b(1, ka, 2, z)
        stage_b(1, ka, 3, z)
        stage_c(1, 0)
        stage_c(1, 1)
        y = _rms(x1_ref[...] + acc_ref[...].T, g_ref[...])
        tile = pl.program_id(0)

        @pl.when(tile < tiles_p)
        def _():
            yp_ref[...] = y

        @pl.when(tile >= tiles_p)
        def _():
            ys_ref[...] = y


def _peer(xn, u_tab, vt_tab, key2, thr, p1, p2, x1, g, t_p):
    t, d = xn.shape
    n_exp = u_tab.shape[0]
    tt, eb = TOKEN_TILE, EXPERT_BLOCK
    lane_tiles = tt // LANES
    n_pairs = n_exp // (2 * eb)
    tab_spec = pl.BlockSpec((PEER_HEADS, lane_tiles, N_KEYS, LANES), lambda i, s: (0, i, 0, 0))
    tiles_p = t_p // tt
    return pl.pallas_call(
        functools.partial(_peer_kernel, tiles_p=tiles_p),
        grid=(t // tt, n_pairs + 1),
        in_specs=[pl.BlockSpec((tt, d), lambda i, s: (i, 0)),
                  pl.BlockSpec((2 * eb, d), lambda i, s: (jnp.minimum(s, n_pairs - 1), 0)),
                  pl.BlockSpec((d, 2 * eb), lambda i, s: (0, jnp.maximum(s - 1, 0))),
                  tab_spec, tab_spec, tab_spec, tab_spec,
                  pl.BlockSpec((tt, d), lambda i, s: (i, 0), pipeline_mode=pl.Buffered(1)),
                  pl.BlockSpec((1, d), lambda i, s: (0, 0))],
        out_specs=_stream_specs(tt, d, tiles_p, pipeline_mode=pl.Buffered(1)),
        out_shape=[jax.ShapeDtypeStruct((t_p, d), F32), jax.ShapeDtypeStruct((t - t_p, d), F32)],
        scratch_shapes=[pltpu.VMEM((d, tt), F32), pltpu.VMEM((eb, tt), F32), pltpu.VMEM((eb, tt), F32),
                        pltpu.VMEM((eb, tt), BF16), pltpu.VMEM((eb, tt), BF16)],
        compiler_params=pltpu.CompilerParams(dimension_semantics=("arbitrary", "arbitrary"),
                                             vmem_limit_bytes=VMEM_LIMIT),
        name="peer",
    )(xn, u_tab, vt_tab, key2, thr, p1, p2, x1, g)


def _ssm_params(lam_re, lam_im, log_step, b_re, b_im, c_re, c_im, d_skip):
    g, p = lam_re.shape
    hh = b_re.shape[2]
    step = jnp.exp(log_step)[:, None]
    decay = jnp.exp(lam_re * step)
    ab_re = decay * jnp.cos(lam_im * step)
    ab_im = decay * jnp.sin(lam_im * step)
    inv = 1.0 / (lam_re * lam_re + lam_im * lam_im)
    k_re = ((ab_re - 1.0) * lam_re + ab_im * lam_im) * inv
    k_im = (ab_im * lam_re - (ab_re - 1.0) * lam_im) * inv
    bb_re = k_re[..., None] * b_re - k_im[..., None] * b_im
    bb_im = k_re[..., None] * b_im + k_im[..., None] * b_re
    eye = jnp.eye(g, dtype=F32)

    def in_mat(bb):
        return jnp.einsum("gph,gk->ghkp", bb, eye).reshape(g * hh, g * p)

    def out_mat(cc):
        return jnp.einsum("ghp,gk->gpkh", cc, eye).reshape(g * p, g * hh)

    wb = jnp.concatenate([in_mat(bb_re), in_mat(bb_im)], axis=1).astype(BF16)
    wc = jnp.concatenate([out_mat(c_re), -out_mat(c_im)], axis=0).astype(BF16)

    a_re = ab_re.reshape(1, g * p)
    a_im = ab_im.reshape(1, g * p)
    pows = [(a_re, a_im)]
    for _ in range(SUBLANES - 1):
        q_re, q_im = pows[-1]
        pows.append((q_re * a_re - q_im * a_im, q_re * a_im + q_im * a_re))
    row = jnp.arange(SUBLANES)[:, None]
    lvl = jnp.stack([jnp.stack([jnp.where(row >= k, pows[k - 1][0], 0.0),
                                jnp.where(row >= k, pows[k - 1][1], 0.0)]) for k in (1, 2, 4)])
    pw = jnp.stack([jnp.concatenate([q[0] for q in pows], axis=0),
                    jnp.concatenate([q[1] for q in pows], axis=0)])
    return wb, wc, d_skip.reshape(1, g * hh), lvl, pw


def kernel(x_prompt, x_sample, state_ssm_re, state_ssm_im, cache_conv, norm_mix, w_in, b_gate,
           ssm_lambda_re, ssm_lambda_im, ssm_log_step, ssm_b_re, ssm_b_im, ssm_c_re, ssm_c_im,
           ssm_d, w_ssm_glu, conv_w, conv_b, conv_ln_g, conv_ln_b, w_conv_out, w_out, norm_ffn,
           peer_w_q, peer_keys1, peer_keys2, peer_u, peer_v, norm_final):
    depth = w_in.shape[0]
    assert depth == 1, "single-layer stack only"
    n_p, l_p, d = x_prompt.shape
    n_s, l_s, _ = x_sample.shape
    assert l_p % CHUNK == 0 and l_s % CHUNK == 0
    t_p, t_s = n_p * l_p, n_s * l_s
    t = t_p + t_s
    assert t_p % TOKEN_TILE == 0 and t_s % TOKEN_TILE == 0
    g, p = ssm_lambda_re.shape[1:]
    n_state = g * p
    d_ssm = g * ssm_b_re.shape[3]
    d_conv = conv_w.shape[2]
    hist_len = CONV_WIDTH - 1

    xp = x_prompt.reshape(t_p, d)
    xs = x_sample.reshape(t_s, d)

    cp, cs = l_p // CHUNK, l_s // CHUNK
    seq_p = jnp.full((n_p, cp), -1, jnp.int32).at[:, 0].set(0).reshape(-1)
    seq_s = jnp.full((n_s, cs), -1, jnp.int32).at[:, 0].set(1 + jnp.arange(n_s, dtype=jnp.int32)).reshape(-1)
    seq = jnp.concatenate([seq_p, seq_s])
    h0 = jnp.concatenate([state_ssm_re[0].reshape(n_s, n_state), state_ssm_im[0].reshape(n_s, n_state)], axis=1)
    h0 = jnp.pad(h0, ((1, 0), (0, 0)))
    hist = jnp.pad(cache_conv[0], ((1, 0), (HIST_ROWS - hist_len, 0), (0, 0)))

    wb, wc, d_skip, lvl, pw = _ssm_params(ssm_lambda_re[0], ssm_lambda_im[0], ssm_log_step[0],
                                          ssm_b_re[0], ssm_b_im[0], ssm_c_re[0], ssm_c_im[0], ssm_d[0])

    u, v, gate = _in_proj(xp, xs, norm_mix, w_in[0].astype(BF16), b_gate, d_ssm, d_conv)
    branch_a, hend = _ssm(seq, u, h0, wb, wc, d_skip, w_ssm_glu[0].astype(BF16), lvl, pw, d)
    cw = jnp.pad(conv_w[0], ((0, HIST_ROWS - CONV_WIDTH), (0, 0)))
    x1 = _mix(seq, v, hist, cw, conv_b, conv_ln_g, conv_ln_b, w_conv_out[0].astype(BF16),
              gate, branch_a, xp, xs, w_out[0].astype(BF16))
    xn2, key2, thr, p1, p2 = _query(x1, norm_ffn, peer_w_q[0].astype(BF16),
                                    peer_keys1[0].astype(BF16), peer_keys2[0].astype(BF16))
    y_p, y_s = _peer(xn2, peer_u[0].astype(BF16), peer_v[0].T.astype(BF16), key2, thr, p1, p2, x1,
                     norm_final.reshape(1, d), t_p)

    y_prompt = y_p.reshape(n_p, l_p, d)
    y_sample = y_s.reshape(n_s, l_s, d)
    last_p = hend[:n_p * cp].reshape(n_p, cp, 2 * n_state)[:, -1]
    last_s = hend[n_p * cp:].reshape(n_s, cs, 2 * n_state)[:, -1]
    v_p = v[:t_p].reshape(n_p, l_p, d_conv)[:, l_p - hist_len:]
    v_s = v[t_p:].reshape(n_s, l_s, d_conv)[:, l_s - hist_len:]
    return (y_prompt, y_sample,
            last_p[:, :n_state].reshape(1, n_p, g, p), last_p[:, n_state:].reshape(1, n_p, g, p), v_p[None],
            last_s[:, :n_state].reshape(1, n_s, g, p), last_s[:, n_state:].reshape(1, n_s, g, p), v_s[None])
```

```python
import functools
import math

import jax
import jax.numpy as jnp
from jax import lax
from jax.experimental import pallas as pl
from jax.experimental.pallas import tpu as pltpu

F32 = jnp.float32
BF16 = jnp.bfloat16

EPS = 1e-6
CHUNK = 64
SSM_GROUP = 16
SSM_STATE = 64
CONV_WIDTH = 31
HIST_ROWS = 32
N_KEYS = 128
PEER_HEADS = 8
PEER_TOPK = 16
LANES = 128
SUBLANES = 8
TOKEN_TILE = 512
EXPERT_BLOCK = 1024
JROWS = 64
VMEM_LIMIT = 58 * 1024 * 1024
NEG_INF = float("-inf")


def _rms(x, g):
    ms = jnp.mean(x * x, axis=-1, keepdims=True)
    return (x * lax.rsqrt(ms + EPS)) * g


def _gelu(x):
    return 0.5 * x * (1.0 + lax.erf(x * (1.0 / math.sqrt(2.0))))


def _sigmoid(x):
    return 1.0 / (1.0 + jnp.exp(-x))


def _const_spec(shape):
    nd = len(shape)
    return pl.BlockSpec(shape, lambda *_: (0,) * nd)


def _stream_specs(tm, d, tiles_p, pipeline_mode=None):
    return [pl.BlockSpec((tm, d), lambda i, *_: (jnp.minimum(i, tiles_p - 1), 0), pipeline_mode=pipeline_mode),
            pl.BlockSpec((tm, d), lambda i, *_: (jnp.maximum(i - tiles_p, 0), 0), pipeline_mode=pipeline_mode)]


def _in_proj_kernel(xp_ref, xs_ref, g_ref, w_ref, bg_ref, u_ref, v_ref, gate_ref, *, d_ssm, d_conv, tiles_p):
    x = jnp.where(pl.program_id(0) < tiles_p, xp_ref[...], xs_ref[...])
    xn = _rms(x, g_ref[...]).astype(BF16)
    u_ref[...] = jnp.dot(xn, w_ref[:, :d_ssm], preferred_element_type=F32)
    pq = jnp.dot(xn, w_ref[:, d_ssm:d_ssm + 2 * d_conv], preferred_element_type=F32)
    v_ref[...] = pq[:, :d_conv] * _sigmoid(pq[:, d_conv:])
    gl = jnp.dot(xn, w_ref[:, d_ssm + 2 * d_conv:], preferred_element_type=F32)
    gate_ref[...] = _sigmoid(gl + bg_ref[...]).astype(BF16)


def _in_proj(xp, xs, g, w_in, b_gate, d_ssm, d_conv):
    d = xp.shape[1]
    t = xp.shape[0] + xs.shape[0]
    d_in = w_in.shape[1]
    n_gate = d_in - d_ssm - 2 * d_conv
    tm = TOKEN_TILE
    tiles_p = xp.shape[0] // tm
    return pl.pallas_call(
        functools.partial(_in_proj_kernel, d_ssm=d_ssm, d_conv=d_conv, tiles_p=tiles_p),
        grid=(t // tm,),
        in_specs=_stream_specs(tm, d, tiles_p) +
                 [_const_spec((1, d)), _const_spec((d, d_in)), _const_spec((1, n_gate))],
        out_specs=[pl.BlockSpec((tm, d_ssm), lambda i: (i, 0)),
                   pl.BlockSpec((tm, d_conv), lambda i: (i, 0)),
                   pl.BlockSpec((tm, n_gate), lambda i: (i, 0))],
        out_shape=[jax.ShapeDtypeStruct((t, d_ssm), F32),
                   jax.ShapeDtypeStruct((t, d_conv), F32),
                   jax.ShapeDtypeStruct((t, n_gate), BF16)],
        compiler_params=pltpu.CompilerParams(dimension_semantics=("arbitrary",),
                                             vmem_limit_bytes=VMEM_LIMIT),
        name="in_proj",
    )(xp, xs, g, w_in, b_gate)


def _ssm_kernel(seq_ref, u_ref, h0_ref, wb_ref, wc_ref, d_ref, wglu_ref, lvl_ref, pw_ref,
                a_ref, hend_ref, bu_ref, cre_ref, cim_ref, *, n_state, d_model):
    tile = u_ref.shape[0]
    chunks = tile // CHUNK
    step = pl.program_id(0)

    @pl.when(step == 0)
    def _():
        cre_ref[...] = jnp.zeros_like(cre_ref)
        cim_ref[...] = jnp.zeros_like(cim_ref)

    u = u_ref[...]
    ub = u.astype(BF16)
    dh = u.shape[1] // 2
    nh = n_state // 2
    for part in (0, n_state):
        for hf in range(2):
            cols = slice(part + hf * nh, part + (hf + 1) * nh)
            bu_ref[:, cols] = jnp.dot(ub[:, hf * dh:(hf + 1) * dh], wb_ref[hf * dh:(hf + 1) * dh, cols],
                                      preferred_element_type=F32)

    for j in range(chunks):
        seq = seq_ref[step * chunks + j]
        start = seq >= 0
        row = jnp.maximum(seq, 0)
        c_re = jnp.where(start, h0_ref[pl.ds(row, 1), :n_state], cre_ref[...])
        c_im = jnp.where(start, h0_ref[pl.ds(row, 1), n_state:], cim_ref[...])

        def block(b, carry, j=j):
            c_re, c_im = carry
            r0 = pl.multiple_of((j * (CHUNK // SUBLANES) + b) * SUBLANES, SUBLANES)
            x_re = bu_ref[pl.ds(r0, SUBLANES), :n_state]
            x_im = bu_ref[pl.ds(r0, SUBLANES), n_state:]
            for lvl, k in enumerate((1, 2, 4)):
                m_re = lvl_ref[lvl, 0]
                m_im = lvl_ref[lvl, 1]
                t_re = pltpu.roll(x_re, k, 0)
                t_im = pltpu.roll(x_im, k, 0)
                x_re, x_im = (x_re + m_re * t_re - m_im * t_im,
                              x_im + m_re * t_im + m_im * t_re)
            p_re = pw_ref[0]
            p_im = pw_ref[1]
            h_re = x_re + p_re * c_re - p_im * c_im
            h_im = x_im + p_re * c_im + p_im * c_re
            bu_ref[pl.ds(r0, SUBLANES), :n_state] = h_re
            bu_ref[pl.ds(r0, SUBLANES), n_state:] = h_im
            return h_re[SUBLANES - 1:], h_im[SUBLANES - 1:]

        c_re, c_im = lax.fori_loop(0, CHUNK // SUBLANES, block, (c_re, c_im))
        cre_ref[...] = c_re
        cim_ref[...] = c_im
        hend_ref[j:j + 1, :n_state] = c_re
        hend_ref[j:j + 1, n_state:] = c_im

    ys = []
    for hf in range(2):
        out = slice(hf * dh, (hf + 1) * dh)
        re = slice(hf * nh, (hf + 1) * nh)
        im = slice(n_state + hf * nh, n_state + (hf + 1) * nh)
        ys.append(jnp.dot(bu_ref[:, re].astype(BF16), wc_ref[re, out], preferred_element_type=F32)
                  + jnp.dot(bu_ref[:, im].astype(BF16), wc_ref[im, out], preferred_element_type=F32))
    y = jnp.concatenate(ys, axis=1) + d_ref[...] * u
    pq =jnp.dot(_gelu(y).astype(BF16), wglu_ref[...], preferred_element_type=F32)
    a_ref[...] = (pq[:, :d_model] * _sigmoid(pq[:, d_model:])).astype(BF16)


def _ssm(seq, u, h0, wb, wc, d_skip, wglu, lvl, pw, d_model):
    t, d_ssm = u.shape
    n_state = wb.shape[1] // 2
    tm = TOKEN_TILE
    chunks = tm // CHUNK
    grid_spec = pltpu.PrefetchScalarGridSpec(
        num_scalar_prefetch=1,
        grid=(t // tm,),
        in_specs=[pl.BlockSpec((tm, d_ssm), lambda i, f: (i, 0)),
                  _const_spec(h0.shape),
                  _const_spec(wb.shape), _const_spec(wc.shape), _const_spec(d_skip.shape),
                  _const_spec(wglu.shape), _const_spec(lvl.shape), _const_spec(pw.shape)],
        out_specs=[pl.BlockSpec((tm, d_model), lambda i, f: (i, 0)),
                   pl.BlockSpec((chunks, 2 * n_state), lambda i, f: (i, 0))],
        scratch_shapes=[pltpu.VMEM((tm, 2 * n_state), F32),
                        pltpu.VMEM((1, n_state), F32), pltpu.VMEM((1, n_state), F32)],
    )
    return pl.pallas_call(
        functools.partial(_ssm_kernel, n_state=n_state, d_model=d_model),
        grid_spec=grid_spec,
        out_shape=[jax.ShapeDtypeStruct((t, d_model), BF16),
                   jax.ShapeDtypeStruct((t // CHUNK, 2 * n_state), F32)],
        compiler_params=pltpu.CompilerParams(dimension_semantics=("arbitrary",),
                                             vmem_limit_bytes=VMEM_LIMIT),
        name="ssm",
    )(seq, u, h0, wb, wc, d_skip, wglu, lvl, pw)


def _mix_kernel(seq_ref, v_ref, hist_ref, cw_ref, cb_ref, lg_ref, lb_ref, wco_ref,
                gate_ref, a_ref, xp_ref, xs_ref, wout_ref, x1_ref, win_ref, shift_ref, act_ref,
                *, d_model, tiles_p):
    tile = v_ref.shape[0]
    chunks = tile // CHUNK
    step = pl.program_id(0)
    lead = HIST_ROWS - (CONV_WIDTH - 1)

    @pl.when(step == 0)
    def _():
        win_ref[...] = jnp.zeros_like(win_ref)

    def chunk_body(j, carry):
        r0 = pl.multiple_of(j * CHUNK, CHUNK)
        seq = seq_ref[step * chunks + j]
        win_ref[:HIST_ROWS] = jnp.where(seq >= 0, hist_ref[jnp.maximum(seq, 0)], win_ref[CHUNK:])
        win_ref[HIST_ROWS:] = v_ref[pl.ds(r0, CHUNK), :]
        span = HIST_ROWS + CHUNK - SUBLANES
        for r in range(1, SUBLANES):
            shift_ref[r - 1] = win_ref[r:r + span, :]
        acc = None
        for k in range(CONV_WIDTH):
            q, r = divmod(lead + k, SUBLANES)
            rows = win_ref[q * SUBLANES:q * SUBLANES + CHUNK, :] if r == 0 else \
                shift_ref[r - 1, q * SUBLANES:q * SUBLANES + CHUNK, :]
            term = cw_ref[k:k + 1, :] * rows
            acc = term if acc is None else acc + term
        acc = acc + cb_ref[...]
        mu = jnp.mean(acc, axis=-1, keepdims=True)
        xc = acc - mu
        ln = (xc * lax.rsqrt(jnp.mean(xc * xc, axis=-1, keepdims=True) + EPS)) * lg_ref[...] + lb_ref[...]
        act_ref[pl.ds(r0, CHUNK), :] = (ln * _sigmoid(ln)).astype(BF16)
        return carry

    lax.fori_loop(0, chunks, chunk_body, 0)
    branch_b = jnp.dot(act_ref[...], wco_ref[...], preferred_element_type=F32)
    ga = gate_ref[:, :d_model].astype(F32)
    gb = gate_ref[:, d_model:].astype(F32)
    m = ga * a_ref[...].astype(F32) + gb * branch_b
    x = jnp.where(step < tiles_p, xp_ref[...], xs_ref[...])
    x1_ref[...] = x + jnp.dot(m.astype(BF16), wout_ref[...], preferred_element_type=F32)


def _mix(seq, v, hist, conv_w, conv_b, ln_g, ln_b, w_conv_out, gate, branch_a, xp, xs, w_out):
    t, d_conv = v.shape
    d_model = xp.shape[1]
    tm = TOKEN_TILE
    tiles_p = xp.shape[0] // tm
    grid_spec = pltpu.PrefetchScalarGridSpec(
        num_scalar_prefetch=1,
        grid=(t // tm,),
        in_specs=[pl.BlockSpec((tm, d_conv), lambda i, f: (i, 0)),
                  _const_spec(hist.shape),
                  _const_spec(conv_w.shape), _const_spec(conv_b.shape), _const_spec(ln_g.shape),
                  _const_spec(ln_b.shape), _const_spec(w_conv_out.shape),
                  pl.BlockSpec((tm, 2 * d_model), lambda i, f: (i, 0)),
                  pl.BlockSpec((tm, d_model), lambda i, f: (i, 0))] +
                 _stream_specs(tm, d_model, tiles_p) + [_const_spec(w_out.shape)],
        out_specs=pl.BlockSpec((tm, d_model), lambda i, f: (i, 0)),
        scratch_shapes=[pltpu.VMEM((HIST_ROWS + CHUNK, d_conv), F32),
                        pltpu.VMEM((SUBLANES - 1, HIST_ROWS + CHUNK - SUBLANES, d_conv), F32),
                        pltpu.VMEM((tm, d_conv), BF16)],
    )
    return pl.pallas_call(
        functools.partial(_mix_kernel, d_model=d_model, tiles_p=tiles_p),
        grid_spec=grid_spec,
        out_shape=jax.ShapeDtypeStruct((t, d_model), F32),
        compiler_params=pltpu.CompilerParams(dimension_semantics=("arbitrary",),
                                             vmem_limit_bytes=VMEM_LIMIT),
        name="mix",
    )(seq, v, hist, conv_w, conv_b, ln_g, ln_b, w_conv_out, gate, branch_a, xp, xs, w_out)


def _top16(s, iota):
    iota16 = iota[:PEER_TOPK]

    def rnd(r, carry):
        work, rank, vals = carry
        rf = jnp.asarray(r, F32)
        m = jnp.max(work, axis=0, keepdims=True)
        first = jnp.min(jnp.where(work == m, iota, float(N_KEYS)), axis=0, keepdims=True)
        sel = iota == first
        rank = jnp.where(sel, rf, rank)
        work = jnp.where(sel, NEG_INF, work)
        vals = jnp.where(iota16 == rf, m, vals)
        return work, rank, vals

    init = (s, jnp.full(s.shape, float(PEER_TOPK), F32), jnp.zeros((PEER_TOPK, s.shape[1]), F32))
    _, rank, vals = lax.fori_loop(0, PEER_TOPK, rnd, init)
    return vals, rank


def _merge16(v1, v2, iota16):
    m0 = v1[0:1] + v2[0:1]

    def rnd(_, carry):
        n, front, z = carry
        m = jnp.max(front, axis=0, keepdims=True)
        a_sel = jnp.min(jnp.where(front == m, iota16, float(PEER_TOPK)), axis=0, keepdims=True)
        sel = iota16 == a_sel
        n = jnp.where(sel, n + 1.0, n)
        k = jnp.max(jnp.where(sel, n, 0.0), axis=0, keepdims=True)
        v2k = jnp.max(jnp.where(iota16 == k, v2, NEG_INF), axis=0, keepdims=True)
        front = jnp.where(sel, v1 + v2k, front)
        return n, front, z + jnp.exp(m - m0)

    init = (jnp.zeros_like(v1), v1 + v2[0:1], jnp.zeros_like(m0))
    n, _, z = lax.fori_loop(0, PEER_TOPK, rnd, init)
    return n, z


def _batcher_network(n):
    def merge(lo, hi, r):
        step = r * 2
        if step < hi - lo:
            yield from merge(lo, hi, step)
            yield from merge(lo + r, hi, step)
            yield from ((i, i + r) for i in range(lo + r, hi - r, step))
        else:
            yield (lo, lo + r)

    def sort(lo, hi):
        if hi - lo >= 1:
            mid = lo + (hi - lo) // 2
            yield from sort(lo, mid)
            yield from sort(mid + 1, hi)
            yield from merge(lo, hi, 1)

    return tuple(sort(0, n - 1))


_SORT16 = _batcher_network(PEER_TOPK)


def _cmpx(x, i, j):
    if x[j] is None:
        return
    if x[i] is None:
        x[i], x[j] = x[j], None
        return
    x[i], x[j] = jnp.maximum(x[i], x[j]), jnp.minimum(x[i], x[j])


def _top16_of_sublane_lists(x):
    n = PEER_TOPK
    for shift in (4, 2, 1):
        y = [None if v is None else pltpu.roll(v, shift, 0) for v in x]
        t = []
        for r in range(n):
            a, b = x[r], y[n - 1 - r]
            t.append(b if a is None else a if b is None else jnp.maximum(a, b))
        for d in (8, 4, 2, 1):
            for r in range(n):
                if r & d == 0:
                    _cmpx(t, r, r | d)
        x = t
    return x


def _sorted_top16(s):
    x = [s[v * SUBLANES:(v + 1) * SUBLANES] for v in range(s.shape[0] // SUBLANES)]
    for i, j in _SORT16:
        _cmpx(x, i, j)
    return _top16_of_sublane_lists(x)


def _count_ge(tiles, bound):
    cnt = jnp.where(tiles[0] >= bound, 1.0, 0.0)
    for v in tiles[1:]:
        cnt = cnt + jnp.where(v >= bound, 1.0, 0.0)
    return jnp.sum(cnt, axis=0, keepdims=True)


def _route_by_value(s1, s2):
    v1 = _sorted_top16(s1)
    v2 = _sorted_top16(s2)
    sub = lax.broadcasted_iota(jnp.int32, (SUBLANES, LANES), 0)

    def by_sublane(vs):
        out = vs[0]
        for r in range(1, SUBLANES):
            out = jnp.where(sub == r, vs[r], out)
        return out

    w1lo, w1hi = by_sublane(v1[:SUBLANES]), by_sublane(v1[SUBLANES:])
    w2lo, w2hi = by_sublane(v2[:SUBLANES]), by_sublane(v2[SUBLANES:])
    cands = [v1[0] + w2lo, v1[0] + w2hi, v1[1] + w2lo,
             jnp.where(sub >= 2, w1lo + v2[0], NEG_INF), w1hi + v2[0],
             jnp.where(sub >= 2, w1lo + v2[1], NEG_INF),
             jnp.where((sub >= 2) & (sub <= 4), v1[2] + w2lo, NEG_INF),
             jnp.where((sub >= 2) & (sub <= 3), v1[3] + w2lo, NEG_INF),
             jnp.where(sub == 2, v1[4] + w2lo, NEG_INF)]
    x = list(cands) + [None] * (PEER_TOPK - len(cands))
    for i, j in _SORT16:
        _cmpx(x, i, j)
    top = _top16_of_sublane_lists(x)
    tau = top[PEER_TOPK - 1]
    z = jnp.exp(top[1] - top[0]) + 1.0
    for r in range(2, PEER_TOPK):
        z = z + jnp.exp(top[r] - top[0])

    s1_tiles = [s1[v * SUBLANES:(v + 1) * SUBLANES] for v in range(N_KEYS // SUBLANES)]
    s2_tiles = [s2[v * SUBLANES:(v + 1) * SUBLANES] for v in range(N_KEYS // SUBLANES)]
    k = float(PEER_TOPK)
    tied = ((_count_ge(s1_tiles, v1[PEER_TOPK - 1]) != k) | (_count_ge(s2_tiles, v2[PEER_TOPK - 1]) != k)
            | (_count_ge(cands, tau) != k))

    n = []
    for a in range(PEER_TOPK):
        na = jnp.where(v1[a] + v2[0] >= tau, 1.0, 0.0)
        for b in range(1, PEER_TOPK // (a + 1)):
            na = na + jnp.where(v1[a] + v2[b] >= tau, 1.0, 0.0)
        n.append(na)

    def pick(vals, bits):
        if len(vals) == 1:
            return vals[0]
        mid = len(vals) // 2
        return jnp.where(bits[0], pick(vals[mid:], bits[1:]), pick(vals[:mid], bits[1:]))

    def rank_bits(x, v):
        b8 = v[7] > x
        b4 = pick([v[3], v[11]], [b8]) > x
        b2 = pick([v[1], v[5], v[9], v[13]], [b8, b4]) > x
        b1 = pick(v[0:PEER_TOPK:2], [b8, b4, b2]) > x
        return [b8, b4, b2, b1], v[PEER_TOPK - 1] > x

    key2_tiles, thr_tiles = [], []
    for x in s2_tiles:
        bits, beyond = rank_bits(x, v2)
        rank = (jnp.where(bits[0], 8.0, 0.0) + jnp.where(bits[1], 4.0, 0.0)
                + jnp.where(bits[2], 2.0, 0.0) + jnp.where(bits[3], 1.0, 0.0))
        key2_tiles.append(jnp.where(beyond, 0.0, k - rank))
    for x in s1_tiles:
        bits, beyond = rank_bits(x, v1)
        thr_tiles.append((k + 1.0) - jnp.where(beyond, 0.0, pick(n, bits)))
    key2 = jnp.concatenate(key2_tiles, axis=0)
    thr = jnp.concatenate(thr_tiles, axis=0)
    p1 =jnp.exp(s1 - v1[0][0:1]) * (1.0 / z[0:1])
    p2 = jnp.exp(s2 - v2[0][0:1])
    return key2, thr, p1, p2, jnp.where(tied, 1.0, 0.0)


def _query_kernel(x1_ref, g_ref, wq_ref, k1_ref, k2_ref,
                  xn_ref, key2_ref, thr_ref, p1_ref, p2_ref, s1_ref, s2_ref, tied_ref):
    tile = x1_ref.shape[0]
    lane_tiles = tile // LANES
    half = k1_ref.shape[1]
    xn = _rms(x1_ref[...], g_ref[...]).astype(BF16)
    xn_ref[...] = xn
    q = jnp.dot(xn, wq_ref[...], preferred_element_type=F32).astype(BF16)
    nt = (((1,), (1,)), ((), ()))
    for h in range(PEER_HEADS):
        q1 = q[:, 2 * h * half:(2 * h + 1) * half]
        q2 = q[:, (2 * h + 1) * half:(2 * h + 2) * half]
        s1 = lax.dot_general(k1_ref[...], q1, nt, preferred_element_type=F32)
        s2 = lax.dot_general(k2_ref[...], q2, nt, preferred_element_type=F32)
        for c in range(lane_tiles):
            s1_ref[h, c] = s1[:, c * LANES:(c + 1) * LANES]
            s2_ref[h, c] = s2[:, c * LANES:(c + 1) * LANES]

    tied_ref[...] = jnp.zeros_like(tied_ref)

    def route_by_value(idx, carry):
        h = idx // lane_tiles
        c = idx % lane_tiles
        key2, thr, p1, p2, tied = _route_by_value(s1_ref[h, c], s2_ref[h, c])
        key2_ref[h, c] = key2
        thr_ref[h, c] = thr
        p1_ref[h, c] = p1
        p2_ref[h, c] = p2
        tied_ref[...] = jnp.maximum(tied_ref[...], tied)
        return carry

    lax.fori_loop(0, PEER_HEADS * lane_tiles, route_by_value, 0)

    def route_by_rank(idx, carry):
        h = idx // lane_tiles
        c = idx % lane_tiles
        iota = lax.broadcasted_iota(jnp.int32, (N_KEYS, LANES), 0).astype(F32)
        s1 = s1_ref[h, c]
        s2 = s2_ref[h, c]
        v1, rank1 = _top16(s1, iota)
        v2, rank2 = _top16(s2, iota)
        n, z = _merge16(v1, v2, iota[:PEER_TOPK])
        cnt = jnp.zeros_like(rank1)
        for a in range(PEER_TOPK):
            cnt = jnp.where(rank1 == float(a), n[a:a + 1], cnt)
        key2_ref[h, c] = float(PEER_TOPK) - rank2
        thr_ref[h, c] = float(PEER_TOPK + 1) - cnt
        p1_ref[h, c] = jnp.exp(s1 - v1[0:1]) / z
        p2_ref[h, c] = jnp.exp(s2 - v2[0:1])
        return carry

    @pl.when(jnp.max(tied_ref[...]) > 0.0)
    def _():
        lax.fori_loop(0, PEER_HEADS * lane_tiles, route_by_rank, 0)


def _query(x1, g, w_q, keys1, keys2):
    t, d = x1.shape
    tm = TOKEN_TILE
    lane_tiles = tm // LANES
    tab_block = (PEER_HEADS, lane_tiles, N_KEYS, LANES)
    tab_spec = pl.BlockSpec(tab_block, lambda i: (0, i, 0, 0))
    tab_shape = jax.ShapeDtypeStruct((PEER_HEADS, t // LANES, N_KEYS, LANES), F32)
    return pl.pallas_call(
        _query_kernel,
        grid=(t // tm,),
        in_specs=[pl.BlockSpec((tm, d), lambda i: (i, 0)), _const_spec(g.shape),
                  _const_spec(w_q.shape), _const_spec(keys1.shape), _const_spec(keys2.shape)],
        out_specs=[pl.BlockSpec((tm, d), lambda i: (i, 0)), tab_spec, tab_spec, tab_spec, tab_spec],
        out_shape=[jax.ShapeDtypeStruct((t, d), BF16), tab_shape, tab_shape, tab_shape, tab_shape],
        scratch_shapes=[pltpu.VMEM(tab_block, F32), pltpu.VMEM(tab_block, F32),
                        pltpu.VMEM((1, LANES), F32)],
        compiler_params=pltpu.CompilerParams(dimension_semantics=("arbitrary",),
                                             vmem_limit_bytes=VMEM_LIMIT),
        name="query",
    )(x1, g, w_q, keys1, keys2)


def _peer_kernel(xn_ref, u_ref, vt_ref, key2_ref, thr_ref, p1_ref, p2_ref, x1_ref, g_ref,
                 yp_ref, ys_ref, acc_ref, act0_ref, act1_ref, coef0_ref, coef1_ref, *, tiles_p):
    s = pl.program_id(1)
    last = pl.num_programs(1) - 1
    eb = u_ref.shape[0] // 2
    tt = xn_ref.shape[0]
    half = tt // 2
    rows = eb // N_KEYS
    nt = (((1,), (1,)), ((), ()))
    act_refs = (act0_ref, act1_ref)
    coef_refs = (coef0_ref, coef1_ref)

    def tail_zero(r):
        return jnp.minimum(jnp.abs(r[r.shape[0] - JROWS:, r.shape[1] - LANES:]), 0.0)

    def stage_a(slot, n):
        r = lax.dot_general(u_ref[slot * eb:(slot + 1) * eb, :], xn_ref[n * half:(n + 1) * half, :], nt,
                            preferred_element_type=F32)
        act_refs[slot][:, n * half:(n + 1) * half] = r
        return tail_zero(r)

    def stage_b(slot, k, c, zero):
        act_ref, coef_ref = act_refs[slot], coef_refs[slot]
        zero_bf = zero.astype(BF16)
        for j0 in range(0, N_KEYS, JROWS):
            ws = [zero_bf] * rows
            for h in range(PEER_HEADS):
                k2 = key2_ref[h, c, j0:j0 + JROWS, :].astype(BF16)
                p2 = p2_ref[h, c, j0:j0 + JROWS, :].astype(BF16)
                for il in range(rows):
                    i = k * rows + il
                    thr = jnp.broadcast_to(thr_ref[h, c, pl.ds(i, 1), :], (JROWS, LANES)).astype(BF16)
                    p1 = jnp.broadcast_to(p1_ref[h, c, pl.ds(i, 1), :], (JROWS, LANES)).astype(BF16)
                    ws[il] = ws[il] + jnp.where(k2 >= thr, p2, jnp.zeros_like(p2)) * p1
            for il in range(rows):
                blk = (slice(il * N_KEYS + j0, il * N_KEYS + j0 + JROWS), slice(c * LANES, (c + 1) * LANES))
                coef_ref[blk] = ws[il] * _gelu(act_ref[blk]).astype(BF16)

    def stage_c(slot, n):
        cols = slice(n * half, (n + 1) * half)
        r = acc_ref[:, cols] + jnp.dot(vt_ref[:, slot * eb:(slot + 1) * eb], coef_refs[slot][:, cols],
                                       preferred_element_type=F32)
        acc_ref[:, cols] = r
        return tail_zero(r)

    zero0 = jnp.zeros((JROWS, LANES), F32)

    @pl.when(s == 0)
    def _():
        acc_ref[...] = jnp.zeros_like(acc_ref)
        z = stage_a(0, 0)
        stage_a(0, 1)
        stage_b(0, 0, 0, z)
        z = stage_a(1, 0)
        stage_b(0, 0, 1, z)
        z = stage_a(1, 1)
        stage_b(0, 0, 2, z)
        stage_b(0, 0, 3, z)

    @pl.when(jnp.logical_and(s > 0, s < last))
    def _():
        ka, kb = 2 * s - 1, 2 * s
        z = stage_a(0, 0)
        stage_b(1, ka, 0, zero0)
        z2 = stage_a(0, 1)
        stage_b(1, ka, 1, z)
        z = stage_c(0, 0)
        stage_b(1, ka, 2, z2)
        z2 = stage_c(0, 1)
        stage_b(1, ka, 3, z)
        z = stage_a(1, 0)
        stage_b(0, kb, 0, z2)
        stage_a(1, 1)
        stage_b(0, kb, 1, z)
        stage_c(1, 0)
        stage_b(0, kb, 2, zero0)
        stage_c(1, 1)
        stage_b(0, kb, 3, zero0)

    @pl.when(s == last)
    def _():
        ka = 2 * s - 1
        stage_c(0, 0)
        stage_b(1, ka, 0, zero0)
        stage_b(1, ka, 1, zero0)
        stage_c(0, 1)
        stage_b(1, ka, 2, zero0)
        stage_c(1, 0)
        stage_b(1, ka, 3, zero0)
        stage_c(1, 1)
        y = _rms(x1_ref[...] + acc_ref[...].T, g_ref[...])
        tile = pl.program_id(0)

        @pl.when(tile < tiles_p)
        def _():
            yp_ref[...] = y

        @pl.when(tile >= tiles_p)
        def _():
            ys_ref[...] = y


def _peer(xn, u_tab, vt_tab, key2, thr, p1, p2, x1, g, t_p):
    t, d = xn.shape
    n_exp = u_tab.shape[0]
    tt, eb = TOKEN_TILE, EXPERT_BLOCK
    lane_tiles = tt // LANES
    n_pairs = n_exp // (2 * eb)
    tab_spec = pl.BlockSpec((PEER_HEADS, lane_tiles, N_KEYS, LANES), lambda i, s: (0, i, 0, 0))
    tiles_p = t_p // tt
    return pl.pallas_call(
        functools.partial(_peer_kernel, tiles_p=tiles_p),
        grid=(t // tt, n_pairs + 1),
        in_specs=[pl.BlockSpec((tt, d), lambda i, s: (i, 0)),
                  pl.BlockSpec((2 * eb, d), lambda i, s: (jnp.minimum(s, n_pairs - 1), 0)),
                  pl.BlockSpec((d, 2 * eb), lambda i, s: (0, jnp.maximum(s - 1, 0))),
                  tab_spec, tab_spec, tab_spec, tab_spec,
                  pl.BlockSpec((tt, d), lambda i, s: (i, 0), pipeline_mode=pl.Buffered(1)),
                  pl.BlockSpec((1, d), lambda i, s: (0, 0))],
        out_specs=_stream_specs(tt, d, tiles_p, pipeline_mode=pl.Buffered(1)),
        out_shape=[jax.ShapeDtypeStruct((t_p, d), F32), jax.ShapeDtypeStruct((t - t_p, d), F32)],
        scratch_shapes=[pltpu.VMEM((d, tt), F32), pltpu.VMEM((eb, tt), F32), pltpu.VMEM((eb, tt), F32),
                        pltpu.VMEM((eb, tt), BF16), pltpu.VMEM((eb, tt), BF16)],
        compiler_params=pltpu.CompilerParams(dimension_semantics=("arbitrary", "arbitrary"),
                                             vmem_limit_bytes=VMEM_LIMIT),
        name="peer",
    )(xn, u_tab, vt_tab, key2, thr, p1, p2, x1, g)


def _ssm_params(lam_re, lam_im, log_step, b_re, b_im, c_re, c_im, d_skip):
    g, p = lam_re.shape
    hh = b_re.shape[2]
    step = jnp.exp(log_step)[:, None]
    decay = jnp.exp(lam_re * step)
    ab_re = decay * jnp.cos(lam_im * step)
    ab_im = decay * jnp.sin(lam_im * step)
    inv = 1.0 / (lam_re * lam_re + lam_im * lam_im)
    k_re = ((ab_re - 1.0) * lam_re + ab_im * lam_im) * inv
    k_im = (ab_im * lam_re - (ab_re - 1.0) * lam_im) * inv
    bb_re = k_re[..., None] * b_re - k_im[..., None] * b_im
    bb_im = k_re[..., None] * b_im + k_im[..., None] * b_re
    eye = jnp.eye(g, dtype=F32)

    def in_mat(bb):
        return jnp.einsum("gph,gk->ghkp", bb, eye).reshape(g * hh, g * p)

    def out_mat(cc):
        return jnp.einsum("ghp,gk->gpkh", cc, eye).reshape(g * p, g * hh)

    wb = jnp.concatenate([in_mat(bb_re), in_mat(bb_im)], axis=1).astype(BF16)
    wc = jnp.concatenate([out_mat(c_re), -out_mat(c_im)], axis=0).astype(BF16)

    a_re = ab_re.reshape(1, g * p)
    a_im = ab_im.reshape(1, g * p)
    pows = [(a_re, a_im)]
    for _ in range(SUBLANES - 1):
        q_re, q_im = pows[-1]
        pows.append((q_re * a_re - q_im * a_im, q_re * a_im + q_im * a_re))
    row = jnp.arange(SUBLANES)[:, None]
    lvl = jnp.stack([jnp.stack([jnp.where(row >= k, pows[k - 1][0], 0.0),
                                jnp.where(row >= k, pows[k - 1][1], 0.0)]) for k in (1, 2, 4)])
    pw = jnp.stack([jnp.concatenate([q[0] for q in pows], axis=0),
                    jnp.concatenate([q[1] for q in pows], axis=0)])
    return wb, wc, d_skip.reshape(1, g * hh), lvl, pw


def kernel(x_prompt, x_sample, state_ssm_re, state_ssm_im, cache_conv, norm_mix, w_in, b_gate,
           ssm_lambda_re, ssm_lambda_im, ssm_log_step, ssm_b_re, ssm_b_im, ssm_c_re, ssm_c_im,
           ssm_d, w_ssm_glu, conv_w, conv_b, conv_ln_g, conv_ln_b, w_conv_out, w_out, norm_ffn,
           peer_w_q, peer_keys1, peer_keys2, peer_u, peer_v, norm_final):
    depth = w_in.shape[0]
    assert depth == 1, "single-layer stack only"
    n_p, l_p, d = x_prompt.shape
    n_s, l_s, _ = x_sample.shape
    assert l_p % CHUNK == 0 and l_s % CHUNK == 0
    t_p, t_s = n_p * l_p, n_s * l_s
    t = t_p + t_s
    assert t_p % TOKEN_TILE == 0 and t_s % TOKEN_TILE == 0
    g, p = ssm_lambda_re.shape[1:]
    n_state = g * p
    d_ssm = g * ssm_b_re.shape[3]
    d_conv = conv_w.shape[2]
    hist_len = CONV_WIDTH - 1

    xp = x_prompt.reshape(t_p, d)
    xs = x_sample.reshape(t_s, d)

    cp, cs = l_p // CHUNK, l_s // CHUNK
    seq_p = jnp.full((n_p, cp), -1, jnp.int32).at[:, 0].set(0).reshape(-1)
    seq_s = jnp.full((n_s, cs), -1, jnp.int32).at[:, 0].set(1 + jnp.arange(n_s, dtype=jnp.int32)).reshape(-1)
    seq = jnp.concatenate([seq_p, seq_s])
    h0 = jnp.concatenate([state_ssm_re[0].reshape(n_s, n_state), state_ssm_im[0].reshape(n_s, n_state)], axis=1)
    h0 = jnp.pad(h0, ((1, 0), (0, 0)))
    hist = jnp.pad(cache_conv[0], ((1, 0), (HIST_ROWS - hist_len, 0), (0, 0)))

    wb, wc, d_skip, lvl, pw = _ssm_params(ssm_lambda_re[0], ssm_lambda_im[0], ssm_log_step[0],
                                          ssm_b_re[0], ssm_b_im[0], ssm_c_re[0], ssm_c_im[0], ssm_d[0])

    u, v, gate = _in_proj(xp, xs, norm_mix, w_in[0].astype(BF16), b_gate, d_ssm, d_conv)
    branch_a, hend = _ssm(seq, u, h0, wb, wc, d_skip, w_ssm_glu[0].astype(BF16), lvl, pw, d)
    cw = jnp.pad(conv_w[0], ((0, HIST_ROWS - CONV_WIDTH), (0, 0)))
    x1 = _mix(seq, v, hist, cw, conv_b, conv_ln_g, conv_ln_b, w_conv_out[0].astype(BF16),
              gate, branch_a, xp, xs, w_out[0].astype(BF16))
    xn2, key2, thr, p1, p2 = _query(x1, norm_ffn, peer_w_q[0].astype(BF16),
                                    peer_keys1[0].astype(BF16), peer_keys2[0].astype(BF16))
    y_p, y_s = _peer(xn2, peer_u[0].astype(BF16), peer_v[0].T.astype(BF16), key2, thr, p1, p2, x1,
                     norm_final.reshape(1, d), t_p)

    y_prompt = y_p.reshape(n_p, l_p, d)
    y_sample = y_s.reshape(n_s, l_s, d)
    last_p = hend[:n_p * cp].reshape(n_p, cp, 2 * n_state)[:, -1]
    last_s = hend[n_p * cp:].reshape(n_s, cs, 2 * n_state)[:, -1]
    v_p = jnp.stack([v[(b + 1) * l_p - hist_len:(b + 1) * l_p] for b in range(n_p)])
    v_s = v[t_p:].reshape(n_s, l_s, d_conv)[:, l_s - hist_len:]
    return (y_prompt, y_sample,
            last_p[:, :n_state].reshape(1, n_p, g, p), last_p[:, n_state:].reshape(1, n_p, g, p), v_p[None],
            last_s[:, :n_state].reshape(1, n_s, g, p), last_s[:, n_state:].reshape(1, n_s, g, p), v_s[None])
```

```python
import functools
import math

import jax
import jax.numpy as jnp
from jax import lax
from jax.experimental import pallas as pl
from jax.experimental.pallas import tpu as pltpu

F32 = jnp.float32
BF16 = jnp.bfloat16

EPS = 1e-6
CHUNK = 64
SSM_GROUP = 16
SSM_STATE = 64
CONV_WIDTH = 31
HIST_ROWS = 32
N_KEYS = 128
PEER_HEADS = 8
PEER_TOPK = 16
LANES = 128
SUBLANES = 8
TOKEN_TILE = 512
EXPERT_BLOCK = 1024
JROWS = 64
VMEM_LIMIT = 58 * 1024 * 1024
NEG_INF = float("-inf")


def _rms(x, g):
    ms = jnp.mean(x * x, axis=-1, keepdims=True)
    return (x * lax.rsqrt(ms + EPS)) * g


def _gelu(x):
    return 0.5 * x * (1.0 + lax.erf(x * (1.0 / math.sqrt(2.0))))


def _sigmoid(x):
    return 1.0 / (1.0 + jnp.exp(-x))


def _const_spec(shape):
    nd = len(shape)
    return pl.BlockSpec(shape, lambda *_: (0,) * nd)


def _stream_specs(tm, d, tiles_p, pipeline_mode=None):
    return [pl.BlockSpec((tm, d), lambda i, *_: (jnp.minimum(i, tiles_p - 1), 0), pipeline_mode=pipeline_mode),
            pl.BlockSpec((tm, d), lambda i, *_: (jnp.maximum(i - tiles_p, 0), 0), pipeline_mode=pipeline_mode)]


def _in_proj_kernel(xp_ref, xs_ref, g_ref, w_ref, bg_ref, u_ref, v_ref, gate_ref, *, d_ssm, d_conv, tiles_p):
    x = jnp.where(pl.program_id(0) < tiles_p, xp_ref[...], xs_ref[...])
    xn = _rms(x, g_ref[...]).astype(BF16)
    u_ref[...] = jnp.dot(xn, w_ref[:, :d_ssm], preferred_element_type=F32)
    pq = jnp.dot(xn, w_ref[:, d_ssm:d_ssm + 2 * d_conv], preferred_element_type=F32)
    v_ref[...] = pq[:, :d_conv] * _sigmoid(pq[:, d_conv:])
    gl = jnp.dot(xn, w_ref[:, d_ssm + 2 * d_conv:], preferred_element_type=F32)
    gate_ref[...] = _sigmoid(gl + bg_ref[...]).astype(BF16)


def _in_proj(xp, xs, g, w_in, b_gate, d_ssm, d_conv):
    d = xp.shape[1]
    t = xp.shape[0] + xs.shape[0]
    d_in = w_in.shape[1]
    n_gate = d_in - d_ssm - 2 * d_conv
    tm = TOKEN_TILE
    tiles_p = xp.shape[0] // tm
    return pl.pallas_call(
        functools.partial(_in_proj_kernel, d_ssm=d_ssm, d_conv=d_conv, tiles_p=tiles_p),
        grid=(t // tm,),
        in_specs=_stream_specs(tm, d, tiles_p) +
                 [_const_spec((1, d)), _const_spec((d, d_in)), _const_spec((1, n_gate))],
        out_specs=[pl.BlockSpec((tm, d_ssm), lambda i: (i, 0)),
                   pl.BlockSpec((tm, d_conv), lambda i: (i, 0)),
                   pl.BlockSpec((tm, n_gate), lambda i: (i, 0))],
        out_shape=[jax.ShapeDtypeStruct((t, d_ssm), F32),
                   jax.ShapeDtypeStruct((t, d_conv), F32),
                   jax.ShapeDtypeStruct((t, n_gate), BF16)],
        compiler_params=pltpu.CompilerParams(dimension_semantics=("arbitrary",),
                                             vmem_limit_bytes=VMEM_LIMIT),
        name="in_proj",
    )(xp, xs, g, w_in, b_gate)


def _ssm_kernel(seq_ref, u_ref, h0_ref, wb_ref, wc_ref, d_ref, wglu_ref, lvl_ref, pw_ref,
                a_ref, hend_ref, bu_ref, cre_ref, cim_ref, *, n_state, d_model):
    tile = u_ref.shape[0]
    chunks = tile // CHUNK
    step = pl.program_id(0)

    @pl.when(step == 0)
    def _():
        cre_ref[...] = jnp.zeros_like(cre_ref)
        cim_ref[...] = jnp.zeros_like(cim_ref)

    u = u_ref[...]
    ub = u.astype(BF16)
    dh = u.shape[1] // 2
    nh = n_state // 2
    for part in (0, n_state):
        for hf in range(2):
            cols = slice(part + hf * nh, part + (hf + 1) * nh)
            bu_ref[:, cols] = jnp.dot(ub[:, hf * dh:(hf + 1) * dh], wb_ref[hf * dh:(hf + 1) * dh, cols],
                                      preferred_element_type=F32)

    for j in range(chunks):
        seq = seq_ref[step * chunks + j]
        start = seq >= 0
        row = jnp.maximum(seq, 0)
        c_re = jnp.where(start, h0_ref[pl.ds(row, 1), :n_state], cre_ref[...])
        c_im = jnp.where(start, h0_ref[pl.ds(row, 1), n_state:], cim_ref[...])

        def block(b, carry, j=j):
            c_re, c_im = carry
            r0 = pl.multiple_of((j * (CHUNK // SUBLANES) + b) * SUBLANES, SUBLANES)
            x_re = bu_ref[pl.ds(r0, SUBLANES), :n_state]
            x_im = bu_ref[pl.ds(r0, SUBLANES), n_state:]
            for lvl, k in enumerate((1, 2, 4)):
                m_re = lvl_ref[lvl, 0]
                m_im = lvl_ref[lvl, 1]
                t_re = pltpu.roll(x_re, k, 0)
                t_im = pltpu.roll(x_im, k, 0)
                x_re, x_im = (x_re + m_re * t_re - m_im * t_im,
                              x_im + m_re * t_im + m_im * t_re)
            p_re = pw_ref[0]
            p_im = pw_ref[1]
            h_re = x_re + p_re * c_re - p_im * c_im
            h_im = x_im + p_re * c_im + p_im * c_re
            bu_ref[pl.ds(r0, SUBLANES), :n_state] = h_re
            bu_ref[pl.ds(r0, SUBLANES), n_state:] = h_im
            return h_re[SUBLANES - 1:], h_im[SUBLANES - 1:]

        c_re, c_im = lax.fori_loop(0, CHUNK // SUBLANES, block, (c_re, c_im))
        cre_ref[...] = c_re
        cim_ref[...] = c_im
        hend_ref[j:j + 1, :n_state] = c_re
        hend_ref[j:j + 1, n_state:] = c_im

    ys = []
    for hf in range(2):
        out = slice(hf * dh, (hf + 1) * dh)
        re = slice(hf * nh, (hf + 1) * nh)
        im = slice(n_state + hf * nh, n_state + (hf + 1) * nh)
        ys.append(jnp.dot(bu_ref[:, re].astype(BF16), wc_ref[re, out], preferred_element_type=F32)
                  + jnp.dot(bu_ref[:, im].astype(BF16), wc_ref[im, out], preferred_element_type=F32))
    y = jnp.concatenate(ys, axis=1) + d_ref[...] * u
    pq =jnp.dot(_gelu(y).astype(BF16), wglu_ref[...], preferred_element_type=F32)
    a_ref[...] = (pq[:, :d_model] * _sigmoid(pq[:, d_model:])).astype(BF16)


def _ssm(seq, u, h0, wb, wc, d_skip, wglu, lvl, pw, d_model):
    t, d_ssm = u.shape
    n_state = wb.shape[1] // 2
    tm = TOKEN_TILE
    chunks = tm // CHUNK
    grid_spec = pltpu.PrefetchScalarGridSpec(
        num_scalar_prefetch=1,
        grid=(t // tm,),
        in_specs=[pl.BlockSpec((tm, d_ssm), lambda i, f: (i, 0)),
                  _const_spec(h0.shape),
                  _const_spec(wb.shape), _const_spec(wc.shape), _const_spec(d_skip.shape),
                  _const_spec(wglu.shape), _const_spec(lvl.shape), _const_spec(pw.shape)],
        out_specs=[pl.BlockSpec((tm, d_model), lambda i, f: (i, 0)),
                   pl.BlockSpec((chunks, 2 * n_state), lambda i, f: (i, 0))],
        scratch_shapes=[pltpu.VMEM((tm, 2 * n_state), F32),
                        pltpu.VMEM((1, n_state), F32), pltpu.VMEM((1, n_state), F32)],
    )
    return pl.pallas_call(
        functools.partial(_ssm_kernel, n_state=n_state, d_model=d_model),
        grid_spec=grid_spec,
        out_shape=[jax.ShapeDtypeStruct((t, d_model), BF16),
                   jax.ShapeDtypeStruct((t // CHUNK, 2 * n_state), F32)],
        compiler_params=pltpu.CompilerParams(dimension_semantics=("arbitrary",),
                                             vmem_limit_bytes=VMEM_LIMIT),
        name="ssm",
    )(seq, u, h0, wb, wc, d_skip, wglu, lvl, pw)


def _mix_kernel(seq_ref, v_ref, hist_ref, cw_ref, cb_ref, lg_ref, lb_ref, wco_ref,
                gate_ref, a_ref, xp_ref, xs_ref, wout_ref, x1_ref, win_ref, shift_ref, act_ref,
                *, d_model, tiles_p):
    tile = v_ref.shape[0]
    chunks = tile // CHUNK
    step = pl.program_id(0)
    lead = HIST_ROWS - (CONV_WIDTH - 1)

    @pl.when(step == 0)
    def _():
        win_ref[...] = jnp.zeros_like(win_ref)

    def chunk_body(j, carry):
        r0 = pl.multiple_of(j * CHUNK, CHUNK)
        seq = seq_ref[step * chunks + j]
        win_ref[:HIST_ROWS] = jnp.where(seq >= 0, hist_ref[jnp.maximum(seq, 0)], win_ref[CHUNK:])
        win_ref[HIST_ROWS:] = v_ref[pl.ds(r0, CHUNK), :]
        span = HIST_ROWS + CHUNK - SUBLANES
        for r in range(1, SUBLANES):
            shift_ref[r - 1] = win_ref[r:r + span, :]
        acc = None
        for k in range(CONV_WIDTH):
            q, r = divmod(lead + k, SUBLANES)
            rows = win_ref[q * SUBLANES:q * SUBLANES + CHUNK, :] if r == 0 else \
                shift_ref[r - 1, q * SUBLANES:q * SUBLANES + CHUNK, :]
            term = cw_ref[k:k + 1, :] * rows
            acc = term if acc is None else acc + term
        acc = acc + cb_ref[...]
        mu = jnp.mean(acc, axis=-1, keepdims=True)
        xc = acc - mu
        ln = (xc * lax.rsqrt(jnp.mean(xc * xc, axis=-1, keepdims=True) + EPS)) * lg_ref[...] + lb_ref[...]
        act_ref[pl.ds(r0, CHUNK), :] = (ln * _sigmoid(ln)).astype(BF16)
        return carry

    lax.fori_loop(0, chunks, chunk_body, 0)
    branch_b = jnp.dot(act_ref[...], wco_ref[...], preferred_element_type=F32)
    ga = gate_ref[:, :d_model].astype(F32)
    gb = gate_ref[:, d_model:].astype(F32)
    m = ga * a_ref[...].astype(F32) + gb * branch_b
    x = jnp.where(step < tiles_p, xp_ref[...], xs_ref[...])
    x1_ref[...] = x + jnp.dot(m.astype(BF16), wout_ref[...], preferred_element_type=F32)


def _mix(seq, v, hist, conv_w, conv_b, ln_g, ln_b, w_conv_out, gate, branch_a, xp, xs, w_out):
    t, d_conv = v.shape
    d_model = xp.shape[1]
    tm = TOKEN_TILE
    tiles_p = xp.shape[0] // tm
    grid_spec = pltpu.PrefetchScalarGridSpec(
        num_scalar_prefetch=1,
        grid=(t // tm,),
        in_specs=[pl.BlockSpec((tm, d_conv), lambda i, f: (i, 0)),
                  _const_spec(hist.shape),
                  _const_spec(conv_w.shape), _const_spec(conv_b.shape), _const_spec(ln_g.shape),
                  _const_spec(ln_b.shape), _const_spec(w_conv_out.shape),
                  pl.BlockSpec((tm, 2 * d_model), lambda i, f: (i, 0)),
                  pl.BlockSpec((tm, d_model), lambda i, f: (i, 0))] +
                 _stream_specs(tm, d_model, tiles_p) + [_const_spec(w_out.shape)],
        out_specs=pl.BlockSpec((tm, d_model), lambda i, f: (i, 0)),
        scratch_shapes=[pltpu.VMEM((HIST_ROWS + CHUNK, d_conv), F32),
                        pltpu.VMEM((SUBLANES - 1, HIST_ROWS + CHUNK - SUBLANES, d_conv), F32),
                        pltpu.VMEM((tm, d_conv), BF16)],
    )
    return pl.pallas_call(
        functools.partial(_mix_kernel, d_model=d_model, tiles_p=tiles_p),
        grid_spec=grid_spec,
        out_shape=jax.ShapeDtypeStruct((t, d_model), F32),
        compiler_params=pltpu.CompilerParams(dimension_semantics=("arbitrary",),
                                             vmem_limit_bytes=VMEM_LIMIT),
        name="mix",
    )(seq, v, hist, conv_w, conv_b, ln_g, ln_b, w_conv_out, gate, branch_a, xp, xs, w_out)


def _top16(s, iota):
    iota16 = iota[:PEER_TOPK]

    def rnd(r, carry):
        work, rank, vals = carry
        rf = jnp.asarray(r, F32)
        m = jnp.max(work, axis=0, keepdims=True)
        first = jnp.min(jnp.where(work == m, iota, float(N_KEYS)), axis=0, keepdims=True)
        sel = iota == first
        rank = jnp.where(sel, rf, rank)
        work = jnp.where(sel, NEG_INF, work)
        vals = jnp.where(iota16 == rf, m, vals)
        return work, rank, vals

    init = (s, jnp.full(s.shape, float(PEER_TOPK), F32), jnp.zeros((PEER_TOPK, s.shape[1]), F32))
    _, rank, vals = lax.fori_loop(0, PEER_TOPK, rnd, init)
    return vals, rank


def _merge16(v1, v2, iota16):
    m0 = v1[0:1] + v2[0:1]

    def rnd(_, carry):
        n, front, z = carry
        m = jnp.max(front, axis=0, keepdims=True)
        a_sel = jnp.min(jnp.where(front == m, iota16, float(PEER_TOPK)), axis=0, keepdims=True)
        sel = iota16 == a_sel
        n = jnp.where(sel, n + 1.0, n)
        k = jnp.max(jnp.where(sel, n, 0.0), axis=0, keepdims=True)
        v2k = jnp.max(jnp.where(iota16 == k, v2, NEG_INF), axis=0, keepdims=True)
        front = jnp.where(sel, v1 + v2k, front)
        return n, front, z + jnp.exp(m - m0)

    init = (jnp.zeros_like(v1), v1 + v2[0:1], jnp.zeros_like(m0))
    n, _, z = lax.fori_loop(0, PEER_TOPK, rnd, init)
    return n, z


def _batcher_network(n):
    def merge(lo, hi, r):
        step = r * 2
        if step < hi - lo:
            yield from merge(lo, hi, step)
            yield from merge(lo + r, hi, step)
            yield from ((i, i + r) for i in range(lo + r, hi - r, step))
        else:
            yield (lo, lo + r)

    def sort(lo, hi):
        if hi - lo >= 1:
            mid = lo + (hi - lo) // 2
            yield from sort(lo, mid)
            yield from sort(mid + 1, hi)
            yield from merge(lo, hi, 1)

    return tuple(sort(0, n - 1))


_SORT16 = _batcher_network(PEER_TOPK)


def _cmpx(x, i, j):
    if x[j] is None:
        return
    if x[i] is None:
        x[i], x[j] = x[j], None
        return
    x[i], x[j] = jnp.maximum(x[i], x[j]), jnp.minimum(x[i], x[j])


def _top16_of_sublane_lists(x):
    n = PEER_TOPK
    for shift in (4, 2, 1):
        y = [None if v is None else pltpu.roll(v, shift, 0) for v in x]
        t = []
        for r in range(n):
            a, b = x[r], y[n - 1 - r]
            t.append(b if a is None else a if b is None else jnp.maximum(a, b))
        for d in (8, 4, 2, 1):
            for r in range(n):
                if r & d == 0:
                    _cmpx(t, r, r | d)
        x = t
    return x


def _sorted_top16(s):
    x = [s[v * SUBLANES:(v + 1) * SUBLANES] for v in range(s.shape[0] // SUBLANES)]
    for i, j in _SORT16:
        _cmpx(x, i, j)
    return _top16_of_sublane_lists(x)


def _count_ge(tiles, bound):
    cnt = jnp.where(tiles[0] >= bound, 1.0, 0.0)
    for v in tiles[1:]:
        cnt = cnt + jnp.where(v >= bound, 1.0, 0.0)
    return jnp.sum(cnt, axis=0, keepdims=True)


def _route_by_value(s1, s2):
    v1 = _sorted_top16(s1)
    v2 = _sorted_top16(s2)
    sub = lax.broadcasted_iota(jnp.int32, (SUBLANES, LANES), 0)

    def by_sublane(vs):
        out = vs[0]
        for r in range(1, SUBLANES):
            out = jnp.where(sub == r, vs[r], out)
        return out

    w1lo, w1hi = by_sublane(v1[:SUBLANES]), by_sublane(v1[SUBLANES:])
    w2lo, w2hi = by_sublane(v2[:SUBLANES]), by_sublane(v2[SUBLANES:])
    cands = [v1[0] + w2lo, v1[0] + w2hi, v1[1] + w2lo,
             jnp.where(sub >= 2, w1lo + v2[0], NEG_INF), w1hi + v2[0],
             jnp.where(sub >= 2, w1lo + v2[1], NEG_INF),
             jnp.where((sub >= 2) & (sub <= 4), v1[2] + w2lo, NEG_INF),
             jnp.where((sub >= 2) & (sub <= 3), v1[3] + w2lo, NEG_INF),
             jnp.where(sub == 2, v1[4] + w2lo, NEG_INF)]
    x = list(cands) + [None] * (PEER_TOPK - len(cands))
    for i, j in _SORT16:
        _cmpx(x, i, j)
    top = _top16_of_sublane_lists(x)
    tau = top[PEER_TOPK - 1]
    z = jnp.exp(top[1] - top[0]) + 1.0
    for r in range(2, PEER_TOPK):
        z = z + jnp.exp(top[r] - top[0])

    s1_tiles = [s1[v * SUBLANES:(v + 1) * SUBLANES] for v in range(N_KEYS // SUBLANES)]
    s2_tiles = [s2[v * SUBLANES:(v + 1) * SUBLANES] for v in range(N_KEYS // SUBLANES)]
    k = float(PEER_TOPK)
    tied = ((_count_ge(s1_tiles, v1[PEER_TOPK - 1]) != k) | (_count_ge(s2_tiles, v2[PEER_TOPK - 1]) != k)
            | (_count_ge(cands, tau) != k))

    n = []
    for a in range(PEER_TOPK):
        na = jnp.where(v1[a] + v2[0] >= tau, 1.0, 0.0)
        for b in range(1, PEER_TOPK // (a + 1)):
            na = na + jnp.where(v1[a] + v2[b] >= tau, 1.0, 0.0)
        n.append(na)

    def pick(vals, bits):
        if len(vals) == 1:
            return vals[0]
        mid = len(vals) // 2
        return jnp.where(bits[0], pick(vals[mid:], bits[1:]), pick(vals[:mid], bits[1:]))

    def rank_bits(x, v):
        b8 = v[7] > x
        b4 = pick([v[3], v[11]], [b8]) > x
        b2 = pick([v[1], v[5], v[9], v[13]], [b8, b4]) > x
        b1 = pick(v[0:PEER_TOPK:2], [b8, b4, b2]) > x
        return [b8, b4, b2, b1], v[PEER_TOPK - 1] > x

    key2_tiles, thr_tiles = [], []
    for x in s2_tiles:
        bits, beyond = rank_bits(x, v2)
        rank = (jnp.where(bits[0], 8.0, 0.0) + jnp.where(bits[1], 4.0, 0.0)
                + jnp.where(bits[2], 2.0, 0.0) + jnp.where(bits[3], 1.0, 0.0))
        key2_tiles.append(jnp.where(beyond, 0.0, k - rank))
    for x in s1_tiles:
        bits, beyond = rank_bits(x, v1)
        thr_tiles.append((k + 1.0) - jnp.where(beyond, 0.0, pick(n, bits)))
    key2 = jnp.concatenate(key2_tiles, axis=0)
    thr = jnp.concatenate(thr_tiles, axis=0)
    p1 =jnp.exp(s1 - v1[0][0:1]) * (1.0 / z[0:1])
    p2 = jnp.exp(s2 - v2[0][0:1])
    return key2, thr, p1, p2, jnp.where(tied, 1.0, 0.0)


def _query_kernel(x1_ref, g_ref, wq_ref, k1_ref, k2_ref,
                  xn_ref, key2_ref, thr_ref, p1_ref, p2_ref, s1_ref, s2_ref, tied_ref):
    tile = x1_ref.shape[0]
    lane_tiles = tile // LANES
    half = k1_ref.shape[1]
    xn = _rms(x1_ref[...], g_ref[...]).astype(BF16)
    xn_ref[...] = xn
    q = jnp.dot(xn, wq_ref[...], preferred_element_type=F32).astype(BF16)
    nt = (((1,), (1,)), ((), ()))
    for h in range(PEER_HEADS):
        q1 = q[:, 2 * h * half:(2 * h + 1) * half]
        q2 = q[:, (2 * h + 1) * half:(2 * h + 2) * half]
        s1 = lax.dot_general(k1_ref[...], q1, nt, preferred_element_type=F32)
        s2 = lax.dot_general(k2_ref[...], q2, nt, preferred_element_type=F32)
        for c in range(lane_tiles):
            s1_ref[h, c] = s1[:, c * LANES:(c + 1) * LANES]
            s2_ref[h, c] = s2[:, c * LANES:(c + 1) * LANES]

    tied_ref[...] = jnp.zeros_like(tied_ref)

    def route_by_value(idx, carry):
        h = idx // lane_tiles
        c = idx % lane_tiles
        key2, thr, p1, p2, tied = _route_by_value(s1_ref[h, c], s2_ref[h, c])
        key2_ref[h, c] = key2
        thr_ref[h, c] = thr
        p1_ref[h, c] = p1
        p2_ref[h, c] = p2
        tied_ref[...] = jnp.maximum(tied_ref[...], tied)
        return carry

    lax.fori_loop(0, PEER_HEADS * lane_tiles, route_by_value, 0)

    def route_by_rank(idx, carry):
        h = idx // lane_tiles
        c = idx % lane_tiles
        iota = lax.broadcasted_iota(jnp.int32, (N_KEYS, LANES), 0).astype(F32)
        s1 = s1_ref[h, c]
        s2 = s2_ref[h, c]
        v1, rank1 = _top16(s1, iota)
        v2, rank2 = _top16(s2, iota)
        n, z = _merge16(v1, v2, iota[:PEER_TOPK])
        cnt = jnp.zeros_like(rank1)
        for a in range(PEER_TOPK):
            cnt = jnp.where(rank1 == float(a), n[a:a + 1], cnt)
        key2_ref[h, c] = float(PEER_TOPK) - rank2
        thr_ref[h, c] = float(PEER_TOPK + 1) - cnt
        p1_ref[h, c] = jnp.exp(s1 - v1[0:1]) / z
        p2_ref[h, c] = jnp.exp(s2 - v2[0:1])
        return carry

    @pl.when(jnp.max(tied_ref[...]) > 0.0)
    def _():
        lax.fori_loop(0, PEER_HEADS * lane_tiles, route_by_rank, 0)


def _query(x1, g, w_q, keys1, keys2):
    t, d = x1.shape
    tm = TOKEN_TILE
    lane_tiles = tm // LANES
    tab_block = (PEER_HEADS, lane_tiles, N_KEYS, LANES)
    tab_spec = pl.BlockSpec(tab_block, lambda i: (0, i, 0, 0))
    tab_shape = jax.ShapeDtypeStruct((PEER_HEADS, t // LANES, N_KEYS, LANES), F32)
    return pl.pallas_call(
        _query_kernel,
        grid=(t // tm,),
        in_specs=[pl.BlockSpec((tm, d), lambda i: (i, 0)), _const_spec(g.shape),
                  _const_spec(w_q.shape), _const_spec(keys1.shape), _const_spec(keys2.shape)],
        out_specs=[pl.BlockSpec((tm, d), lambda i: (i, 0)), tab_spec, tab_spec, tab_spec, tab_spec],
        out_shape=[jax.ShapeDtypeStruct((t, d), BF16), tab_shape, tab_shape, tab_shape, tab_shape],
        scratch_shapes=[pltpu.VMEM(tab_block, F32), pltpu.VMEM(tab_block, F32),
                        pltpu.VMEM((1, LANES), F32)],
        compiler_params=pltpu.CompilerParams(dimension_semantics=("arbitrary",),
                                             vmem_limit_bytes=VMEM_LIMIT),
        name="query",
    )(x1, g, w_q, keys1, keys2)


def _peer_kernel(xn_ref, u_ref, vt_ref, key2_ref, thr_ref, p1_ref, p2_ref, x1_ref, g_ref,
                 yp_ref, ys_ref, acc_ref, act0_ref, act1_ref, coef0_ref, coef1_ref, *, tiles_p):
    s = pl.program_id(1)
    last = pl.num_programs(1) - 1
    eb = u_ref.shape[0] // 2
    tt = xn_ref.shape[0]
    half = tt // 2
    rows = eb // N_KEYS
    nt = (((1,), (1,)), ((), ()))
    act_refs = (act0_ref, act1_ref)
    coef_refs = (coef0_ref, coef1_ref)

    def tail_zero(r):
        return jnp.minimum(jnp.abs(r[r.shape[0] - JROWS:, r.shape[1] - LANES:]), 0.0)

    def stage_a(slot, n):
        r = lax.dot_general(u_ref[slot * eb:(slot + 1) * eb, :], xn_ref[n * half:(n + 1) * half, :], nt,
                            preferred_element_type=F32)
        act_refs[slot][:, n * half:(n + 1) * half] = r
        return tail_zero(r)

    def stage_b(slot, k, c, zero):
        act_ref, coef_ref = act_refs[slot], coef_refs[slot]
        zero_bf = zero.astype(BF16)
        for j0 in range(0, N_KEYS, JROWS):
            ws = [zero_bf] * rows
            for h in range(PEER_HEADS):
                k2 = key2_ref[h, c, j0:j0 + JROWS, :].astype(BF16)
                p2 = p2_ref[h, c, j0:j0 + JROWS, :].astype(BF16)
                for il in range(rows):
                    i = k * rows + il
                    thr = jnp.broadcast_to(thr_ref[h, c, pl.ds(i, 1), :], (JROWS, LANES)).astype(BF16)
                    p1 = jnp.broadcast_to(p1_ref[h, c, pl.ds(i, 1), :], (JROWS, LANES)).astype(BF16)
                    ws[il] = ws[il] + jnp.where(k2 >= thr, p2, jnp.zeros_like(p2)) * p1
            for il in range(rows):
                blk = (slice(il * N_KEYS + j0, il * N_KEYS + j0 + JROWS), slice(c * LANES, (c + 1) * LANES))
                coef_ref[blk] = ws[il] * _gelu(act_ref[blk].astype(BF16))

    def stage_c(slot, n):
        cols = slice(n * half, (n + 1) * half)
        r = acc_ref[:, cols] + jnp.dot(vt_ref[:, slot * eb:(slot + 1) * eb], coef_refs[slot][:, cols],
                                       preferred_element_type=F32)
        acc_ref[:, cols] = r
        return tail_zero(r)

    zero0 = jnp.zeros((JROWS, LANES), F32)

    @pl.when(s == 0)
    def _():
        acc_ref[...] = jnp.zeros_like(acc_ref)
        z = stage_a(0, 0)
        stage_a(0, 1)
        stage_b(0, 0, 0, z)
        z = stage_a(1, 0)
        stage_b(0, 0, 1, z)
        z = stage_a(1, 1)
        stage_b(0, 0, 2, z)
        stage_b(0, 0, 3, z)

    @pl.when(jnp.logical_and(s > 0, s < last))
    def _():
        ka, kb = 2 * s - 1, 2 * s
        z = stage_a(0, 0)
        stage_b(1, ka, 0, zero0)
        z2 = stage_a(0, 1)
        stage_b(1, ka, 1, z)
        z = stage_c(0, 0)
        stage_b(1, ka, 2, z2)
        z2 = stage_c(0, 1)
        stage_b(1, ka, 3, z)
        z = stage_a(1, 0)
        stage_b(0, kb, 0, z2)
        z2 = stage_a(1, 1)
        stage_b(0, kb, 1, z)
        z = stage_c(1, 0)
        stage_b(0, kb, 2, z2)
        stage_c(1, 1)
        stage_b(0, kb, 3, z)

    @pl.when(s == last)
    def _():
        ka = 2 * s - 1
        z = stage_c(0, 0)
        stage_b(1, ka, 0, zero0)
        stage_b(1, ka, 1, z)
        z = stage_c(0, 1)
        stage_b(1, ka, 2, z)
        stage_b(1, ka, 3, z)
        stage_c(1, 0)
        stage_c(1, 1)
        y = _rms(x1_ref[...] + acc_ref[...].T, g_ref[...])
        tile = pl.program_id(0)

        @pl.when(tile < tiles_p)
        def _():
            yp_ref[...] = y

        @pl.when(tile >= tiles_p)
        def _():
            ys_ref[...] = y


def _peer(xn, u_tab, vt_tab, key2, thr, p1, p2, x1, g, t_p):
    t, d = xn.shape
    n_exp = u_tab.shape[0]
    tt, eb = TOKEN_TILE, EXPERT_BLOCK
    lane_tiles = tt // LANES
    n_pairs = n_exp // (2 * eb)
    tab_spec = pl.BlockSpec((PEER_HEADS, lane_tiles, N_KEYS, LANES), lambda i, s: (0, i, 0, 0))
    tiles_p = t_p // tt
    return pl.pallas_call(
        functools.partial(_peer_kernel, tiles_p=tiles_p),
        grid=(t // tt, n_pairs + 1),
        in_specs=[pl.BlockSpec((tt, d), lambda i, s: (i, 0)),
                  pl.BlockSpec((2 * eb, d), lambda i, s: (jnp.minimum(s, n_pairs - 1), 0)),
                  pl.BlockSpec((d, 2 * eb), lambda i, s: (0, jnp.maximum(s - 1, 0))),
                  tab_spec, tab_spec, tab_spec, tab_spec,
                  pl.BlockSpec((tt, d), lambda i, s: (i, 0), pipeline_mode=pl.Buffered(1)),
                  pl.BlockSpec((1, d), lambda i, s: (0, 0))],
        out_specs=_stream_specs(tt, d, tiles_p, pipeline_mode=pl.Buffered(1)),
        out_shape=[jax.ShapeDtypeStruct((t_p, d), F32), jax.ShapeDtypeStruct((t - t_p, d), F32)],
        scratch_shapes=[pltpu.VMEM((d, tt), F32), pltpu.VMEM((eb, tt), F32), pltpu.VMEM((eb, tt), F32),
                        pltpu.VMEM((eb, tt), BF16), pltpu.VMEM((eb, tt), BF16)],
        compiler_params=pltpu.CompilerParams(dimension_semantics=("arbitrary", "arbitrary"),
                                             vmem_limit_bytes=VMEM_LIMIT),
        name="peer",
    )(xn, u_tab, vt_tab, key2, thr, p1, p2, x1, g)


def _ssm_params(lam_re, lam_im, log_step, b_re, b_im, c_re, c_im, d_skip):
    g, p = lam_re.shape
    hh = b_re.shape[2]
    step = jnp.exp(log_step)[:, None]
    decay = jnp.exp(lam_re * step)
    ab_re = decay * jnp.cos(lam_im * step)
    ab_im = decay * jnp.sin(lam_im * step)
    inv = 1.0 / (lam_re * lam_re + lam_im * lam_im)
    k_re = ((ab_re - 1.0) * lam_re + ab_im * lam_im) * inv
    k_im = (ab_im * lam_re - (ab_re - 1.0) * lam_im) * inv
    bb_re = k_re[..., None] * b_re - k_im[..., None] * b_im
    bb_im = k_re[..., None] * b_im + k_im[..., None] * b_re
    eye = jnp.eye(g, dtype=F32)

    def in_mat(bb):
        return jnp.einsum("gph,gk->ghkp", bb, eye).reshape(g * hh, g * p)

    def out_mat(cc):
        return jnp.einsum("ghp,gk->gpkh", cc, eye).reshape(g * p, g * hh)

    wb = jnp.concatenate([in_mat(bb_re), in_mat(bb_im)], axis=1).astype(BF16)
    wc = jnp.concatenate([out_mat(c_re), -out_mat(c_im)], axis=0).astype(BF16)

    a_re = ab_re.reshape(1, g * p)
    a_im = ab_im.reshape(1, g * p)
    pows = [(a_re, a_im)]
    for _ in range(SUBLANES - 1):
        q_re, q_im = pows[-1]
        pows.append((q_re * a_re - q_im * a_im, q_re * a_im + q_im * a_re))
    row = jnp.arange(SUBLANES)[:, None]
    lvl = jnp.stack([jnp.stack([jnp.where(row >= k, pows[k - 1][0], 0.0),
                                jnp.where(row >= k, pows[k - 1][1], 0.0)]) for k in (1, 2, 4)])
    pw = jnp.stack([jnp.concatenate([q[0] for q in pows], axis=0),
                    jnp.concatenate([q[1] for q in pows], axis=0)])
    return wb, wc, d_skip.reshape(1, g * hh), lvl, pw


def kernel(x_prompt, x_sample, state_ssm_re, state_ssm_im, cache_conv, norm_mix, w_in, b_gate,
           ssm_lambda_re, ssm_lambda_im, ssm_log_step, ssm_b_re, ssm_b_im, ssm_c_re, ssm_c_im,
           ssm_d, w_ssm_glu, conv_w, conv_b, conv_ln_g, conv_ln_b, w_conv_out, w_out, norm_ffn,
           peer_w_q, peer_keys1, peer_keys2, peer_u, peer_v, norm_final):
    depth = w_in.shape[0]
    assert depth == 1, "single-layer stack only"
    n_p, l_p, d = x_prompt.shape
    n_s, l_s, _ = x_sample.shape
    assert l_p % CHUNK == 0 and l_s % CHUNK == 0
    t_p, t_s = n_p * l_p, n_s * l_s
    t = t_p + t_s
    assert t_p % TOKEN_TILE == 0 and t_s % TOKEN_TILE == 0
    g, p = ssm_lambda_re.shape[1:]
    n_state = g * p
    d_ssm = g * ssm_b_re.shape[3]
    d_conv = conv_w.shape[2]
    hist_len = CONV_WIDTH - 1

    xp = x_prompt.reshape(t_p, d)
    xs = x_sample.reshape(t_s, d)

    cp, cs = l_p // CHUNK, l_s // CHUNK
    seq_p = jnp.full((n_p, cp), -1, jnp.int32).at[:, 0].set(0).reshape(-1)
    seq_s = jnp.full((n_s, cs), -1, jnp.int32).at[:, 0].set(1 + jnp.arange(n_s, dtype=jnp.int32)).reshape(-1)
    seq = jnp.concatenate([seq_p, seq_s])
    h0 = jnp.concatenate([state_ssm_re[0].reshape(n_s, n_state), state_ssm_im[0].reshape(n_s, n_state)], axis=1)
    h0 = jnp.pad(h0, ((1, 0), (0, 0)))
    hist = jnp.pad(cache_conv[0], ((1, 0), (HIST_ROWS - hist_len, 0), (0, 0)))

    wb, wc, d_skip, lvl, pw = _ssm_params(ssm_lambda_re[0], ssm_lambda_im[0], ssm_log_step[0],
                                          ssm_b_re[0], ssm_b_im[0], ssm_c_re[0], ssm_c_im[0], ssm_d[0])

    u, v, gate = _in_proj(xp, xs, norm_mix, w_in[0].astype(BF16), b_gate, d_ssm, d_conv)
    branch_a, hend = _ssm(seq, u, h0, wb, wc, d_skip, w_ssm_glu[0].astype(BF16), lvl, pw, d)
    cw = jnp.pad(conv_w[0], ((0, HIST_ROWS - CONV_WIDTH), (0, 0)))
    x1 = _mix(seq, v, hist, cw, conv_b, conv_ln_g, conv_ln_b, w_conv_out[0].astype(BF16),
              gate, branch_a, xp, xs, w_out[0].astype(BF16))
    xn2, key2, thr, p1, p2 = _query(x1, norm_ffn, peer_w_q[0].astype(BF16),
                                    peer_keys1[0].astype(BF16), peer_keys2[0].astype(BF16))
    y_p, y_s = _peer(xn2, peer_u[0].astype(BF16), peer_v[0].astype(BF16).T, key2, thr, p1, p2, x1,
                     norm_final.reshape(1, d), t_p)

    y_prompt = y_p.reshape(n_p, l_p, d)
    y_sample = y_s.reshape(n_s, l_s, d)
    last_p = hend[:n_p * cp].reshape(n_p, cp, 2 * n_state)[:, -1]
    last_s = hend[n_p * cp:].reshape(n_s, cs, 2 * n_state)[:, -1]
    v_p = jnp.stack([v[(b + 1) * l_p - hist_len:(b + 1) * l_p] for b in range(n_p)])
    v_s = v[t_p:].reshape(n_s, l_s, d_conv)[:, l_s - hist_len:]
    return (y_prompt, y_sample,
            last_p[:, :n_state].reshape(1, n_p, g, p), last_p[:, n_state:].reshape(1, n_p, g, p), v_p[None],
            last_s[:, :n_state].reshape(1, n_s, g, p), last_s[:, n_state:].reshape(1, n_s, g, p), v_s[None])
```

```python
import functools
import math

import jax
import jax.numpy as jnp
from jax import lax
from jax.experimental import pallas as pl
from jax.experimental.pallas import tpu as pltpu

F32 = jnp.float32
BF16 = jnp.bfloat16

EPS = 1e-6
CHUNK = 64
SSM_GROUP = 16
SSM_STATE = 64
CONV_WIDTH = 31
HIST_ROWS = 32
N_KEYS = 128
PEER_HEADS = 8
PEER_TOPK = 16
LANES = 128
SUBLANES = 8
TOKEN_TILE = 512
EXPERT_BLOCK = 1024
JROWS = 64
VMEM_LIMIT = 58 * 1024 * 1024
NEG_INF = float("-inf")


def _rms(x, g):
    ms = jnp.mean(x * x, axis=-1, keepdims=True)
    return (x * lax.rsqrt(ms + EPS)) * g


def _gelu(x):
    return 0.5 * x * (1.0 + lax.erf(x * (1.0 / math.sqrt(2.0))))


def _sigmoid(x):
    return 1.0 / (1.0 + jnp.exp(-x))


def _const_spec(shape):
    nd = len(shape)
    return pl.BlockSpec(shape, lambda *_: (0,) * nd)


def _stream_specs(tm, d, tiles_p, pipeline_mode=None):
    return [pl.BlockSpec((tm, d), lambda i, *_: (jnp.minimum(i, tiles_p - 1), 0), pipeline_mode=pipeline_mode),
            pl.BlockSpec((tm, d), lambda i, *_: (jnp.maximum(i - tiles_p, 0), 0), pipeline_mode=pipeline_mode)]


def _in_proj_kernel(xp_ref, xs_ref, g_ref, w_ref, bg_ref, u_ref, v_ref, gate_ref, *, d_ssm, d_conv, tiles_p):
    x = jnp.where(pl.program_id(0) < tiles_p, xp_ref[...], xs_ref[...])
    xn = _rms(x, g_ref[...]).astype(BF16)
    u_ref[...] = jnp.dot(xn, w_ref[:, :d_ssm], preferred_element_type=F32)
    pq = jnp.dot(xn, w_ref[:, d_ssm:d_ssm + 2 * d_conv], preferred_element_type=F32)
    v_ref[...] = pq[:, :d_conv] * _sigmoid(pq[:, d_conv:])
    gl = jnp.dot(xn, w_ref[:, d_ssm + 2 * d_conv:], preferred_element_type=F32)
    gate_ref[...] = _sigmoid(gl + bg_ref[...]).astype(BF16)


def _in_proj(xp, xs, g, w_in, b_gate, d_ssm, d_conv):
    d = xp.shape[1]
    t = xp.shape[0] + xs.shape[0]
    d_in = w_in.shape[1]
    n_gate = d_in - d_ssm - 2 * d_conv
    tm = TOKEN_TILE
    tiles_p = xp.shape[0] // tm
    return pl.pallas_call(
        functools.partial(_in_proj_kernel, d_ssm=d_ssm, d_conv=d_conv, tiles_p=tiles_p),
        grid=(t // tm,),
        in_specs=_stream_specs(tm, d, tiles_p) +
                 [_const_spec((1, d)), _const_spec((d, d_in)), _const_spec((1, n_gate))],
        out_specs=[pl.BlockSpec((tm, d_ssm), lambda i: (i, 0)),
                   pl.BlockSpec((tm, d_conv), lambda i: (i, 0)),
                   pl.BlockSpec((tm, n_gate), lambda i: (i, 0))],
        out_shape=[jax.ShapeDtypeStruct((t, d_ssm), F32),
                   jax.ShapeDtypeStruct((t, d_conv), F32),
                   jax.ShapeDtypeStruct((t, n_gate), BF16)],
        compiler_params=pltpu.CompilerParams(dimension_semantics=("arbitrary",),
                                             vmem_limit_bytes=VMEM_LIMIT),
        name="in_proj",
    )(xp, xs, g, w_in, b_gate)


def _ssm_kernel(seq_ref, u_ref, h0_ref, wb_ref, wc_ref, d_ref, wglu_ref, lvl_ref, pw_ref,
                a_ref, hend_ref, bu_ref, cre_ref, cim_ref, *, n_state, d_model):
    tile = u_ref.shape[0]
    chunks = tile // CHUNK
    step = pl.program_id(0)

    @pl.when(step == 0)
    def _():
        cre_ref[...] = jnp.zeros_like(cre_ref)
        cim_ref[...] = jnp.zeros_like(cim_ref)

    u = u_ref[...]
    ub = u.astype(BF16)
    dh = u.shape[1] // 2
    nh = n_state // 2
    for part in (0, n_state):
        for hf in range(2):
            cols = slice(part + hf * nh, part + (hf + 1) * nh)
            bu_ref[:, cols] = jnp.dot(ub[:, hf * dh:(hf + 1) * dh], wb_ref[hf * dh:(hf + 1) * dh, cols],
                                      preferred_element_type=F32)

    for j in range(chunks):
        seq = seq_ref[step * chunks + j]
        start = seq >= 0
        row = jnp.maximum(seq, 0)
        c_re = jnp.where(start, h0_ref[pl.ds(row, 1), :n_state], cre_ref[...])
        c_im = jnp.where(start, h0_ref[pl.ds(row, 1), n_state:], cim_ref[...])

        def block(b, carry, j=j):
            c_re, c_im = carry
            r0 = pl.multiple_of((j * (CHUNK // SUBLANES) + b) * SUBLANES, SUBLANES)
            x_re = bu_ref[pl.ds(r0, SUBLANES), :n_state]
            x_im = bu_ref[pl.ds(r0, SUBLANES), n_state:]
            for lvl, k in enumerate((1, 2, 4)):
                m_re = lvl_ref[lvl, 0]
                m_im = lvl_ref[lvl, 1]
                t_re = pltpu.roll(x_re, k, 0)
                t_im = pltpu.roll(x_im, k, 0)
                x_re, x_im = (x_re + m_re * t_re - m_im * t_im,
                              x_im + m_re * t_im + m_im * t_re)
            p_re = pw_ref[0]
            p_im = pw_ref[1]
            h_re = x_re + p_re * c_re - p_im * c_im
            h_im = x_im + p_re * c_im + p_im * c_re
            bu_ref[pl.ds(r0, SUBLANES), :n_state] = h_re
            bu_ref[pl.ds(r0, SUBLANES), n_state:] = h_im
            return h_re[SUBLANES - 1:], h_im[SUBLANES - 1:]

        c_re, c_im = lax.fori_loop(0, CHUNK // SUBLANES, block, (c_re, c_im))
        cre_ref[...] = c_re
        cim_ref[...] = c_im
        hend_ref[j:j + 1, :n_state] = c_re
        hend_ref[j:j + 1, n_state:] = c_im

    ys = []
    for hf in range(2):
        out = slice(hf * dh, (hf + 1) * dh)
        re = slice(hf * nh, (hf + 1) * nh)
        im = slice(n_state + hf * nh, n_state + (hf + 1) * nh)
        ys.append(jnp.dot(bu_ref[:, re].astype(BF16), wc_ref[re, out], preferred_element_type=F32)
                  + jnp.dot(bu_ref[:, im].astype(BF16), wc_ref[im, out], preferred_element_type=F32))
    y = jnp.concatenate(ys, axis=1) + d_ref[...] * u
    pq =jnp.dot(_gelu(y).astype(BF16), wglu_ref[...], preferred_element_type=F32)
    a_ref[...] = (pq[:, :d_model] * _sigmoid(pq[:, d_model:])).astype(BF16)


def _ssm(seq, u, h0, wb, wc, d_skip, wglu, lvl, pw, d_model):
    t, d_ssm = u.shape
    n_state = wb.shape[1] // 2
    tm = TOKEN_TILE
    chunks = tm // CHUNK
    grid_spec = pltpu.PrefetchScalarGridSpec(
        num_scalar_prefetch=1,
        grid=(t // tm,),
        in_specs=[pl.BlockSpec((tm, d_ssm), lambda i, f: (i, 0)),
                  _const_spec(h0.shape),
                  _const_spec(wb.shape), _const_spec(wc.shape), _const_spec(d_skip.shape),
                  _const_spec(wglu.shape), _const_spec(lvl.shape), _const_spec(pw.shape)],
        out_specs=[pl.BlockSpec((tm, d_model), lambda i, f: (i, 0)),
                   pl.BlockSpec((chunks, 2 * n_state), lambda i, f: (i, 0))],
        scratch_shapes=[pltpu.VMEM((tm, 2 * n_state), F32),
                        pltpu.VMEM((1, n_state), F32), pltpu.VMEM((1, n_state), F32)],
    )
    return pl.pallas_call(
        functools.partial(_ssm_kernel, n_state=n_state, d_model=d_model),
        grid_spec=grid_spec,
        out_shape=[jax.ShapeDtypeStruct((t, d_model), BF16),
                   jax.ShapeDtypeStruct((t // CHUNK, 2 * n_state), F32)],
        compiler_params=pltpu.CompilerParams(dimension_semantics=("arbitrary",),
                                             vmem_limit_bytes=VMEM_LIMIT),
        name="ssm",
    )(seq, u, h0, wb, wc, d_skip, wglu, lvl, pw)


def _mix_kernel(seq_ref, v_ref, hist_ref, cw_ref, cb_ref, lg_ref, lb_ref, wco_ref,
                gate_ref, a_ref, xp_ref, xs_ref, wout_ref, x1_ref, win_ref, shift_ref, conv_ref,
                *, d_model, tiles_p):
    tile = v_ref.shape[0]
    chunks = tile // CHUNK
    step = pl.program_id(0)
    lead = HIST_ROWS - (CONV_WIDTH - 1)

    @pl.when(step == 0)
    def _():
        win_ref[...] = jnp.zeros_like(win_ref)

    def chunk_body(j, carry):
        r0 = pl.multiple_of(j * CHUNK, CHUNK)
        seq = seq_ref[step * chunks + j]
        win_ref[:HIST_ROWS] = jnp.where(seq >= 0, hist_ref[jnp.maximum(seq, 0)], win_ref[CHUNK:])
        win_ref[HIST_ROWS:] = v_ref[pl.ds(r0, CHUNK), :]
        span = HIST_ROWS + CHUNK - SUBLANES
        for r in range(1, SUBLANES):
            shift_ref[r - 1] = win_ref[r:r + span, :]
        acc = None
        for k in range(CONV_WIDTH):
            q, r = divmod(lead + k, SUBLANES)
            rows = win_ref[q * SUBLANES:q * SUBLANES + CHUNK, :] if r == 0 else \
                shift_ref[r - 1, q * SUBLANES:q * SUBLANES + CHUNK, :]
            term = cw_ref[k:k + 1, :] * rows
            acc = term if acc is None else acc + term
        conv_ref[pl.ds(r0, CHUNK), :] = acc + cb_ref[...]
        return carry

    lax.fori_loop(0, chunks, chunk_body, 0)
    conv = conv_ref[...]
    mu = jnp.mean(conv, axis=-1, keepdims=True)
    xc = conv - mu
    ln = (xc * lax.rsqrt(jnp.mean(xc * xc, axis=-1, keepdims=True) + EPS)) * lg_ref[...] + lb_ref[...]
    branch_b = jnp.dot((ln * _sigmoid(ln)).astype(BF16), wco_ref[...], preferred_element_type=F32)
    ga = gate_ref[:, :d_model].astype(F32)
    gb = gate_ref[:, d_model:].astype(F32)
    m = ga * a_ref[...].astype(F32) + gb * branch_b
    x = jnp.where(step < tiles_p, xp_ref[...], xs_ref[...])
    x1_ref[...] = x + jnp.dot(m.astype(BF16), wout_ref[...], preferred_element_type=F32)


def _mix(seq, v, hist, conv_w, conv_b, ln_g, ln_b, w_conv_out, gate, branch_a, xp, xs, w_out):
    t, d_conv = v.shape
    d_model = xp.shape[1]
    tm = TOKEN_TILE
    tiles_p = xp.shape[0] // tm
    grid_spec = pltpu.PrefetchScalarGridSpec(
        num_scalar_prefetch=1,
        grid=(t // tm,),
        in_specs=[pl.BlockSpec((tm, d_conv), lambda i, f: (i, 0)),
                  _const_spec(hist.shape),
                  _const_spec(conv_w.shape), _const_spec(conv_b.shape), _const_spec(ln_g.shape),
                  _const_spec(ln_b.shape), _const_spec(w_conv_out.shape),
                  pl.BlockSpec((tm, 2 * d_model), lambda i, f: (i, 0)),
                  pl.BlockSpec((tm, d_model), lambda i, f: (i, 0))] +
                 _stream_specs(tm, d_model, tiles_p) + [_const_spec(w_out.shape)],
        out_specs=pl.BlockSpec((tm, d_model), lambda i, f: (i, 0)),
        scratch_shapes=[pltpu.VMEM((HIST_ROWS + CHUNK, d_conv), F32),
                        pltpu.VMEM((SUBLANES - 1, HIST_ROWS + CHUNK - SUBLANES, d_conv), F32),
                        pltpu.VMEM((tm, d_conv), F32)],
    )
    return pl.pallas_call(
        functools.partial(_mix_kernel, d_model=d_model, tiles_p=tiles_p),
        grid_spec=grid_spec,
        out_shape=jax.ShapeDtypeStruct((t, d_model), F32),
        compiler_params=pltpu.CompilerParams(dimension_semantics=("arbitrary",),
                                             vmem_limit_bytes=VMEM_LIMIT),
        name="mix",
    )(seq, v, hist, conv_w, conv_b, ln_g, ln_b, w_conv_out, gate, branch_a, xp, xs, w_out)


def _top16(s, iota):
    iota16 = iota[:PEER_TOPK]

    def rnd(r, carry):
        work, rank, vals = carry
        rf = jnp.asarray(r, F32)
        m = jnp.max(work, axis=0, keepdims=True)
        first = jnp.min(jnp.where(work == m, iota, float(N_KEYS)), axis=0, keepdims=True)
        sel = iota == first
        rank = jnp.where(sel, rf, rank)
        work = jnp.where(sel, NEG_INF, work)
        vals = jnp.where(iota16 == rf, m, vals)
        return work, rank, vals

    init = (s, jnp.full(s.shape, float(PEER_TOPK), F32), jnp.zeros((PEER_TOPK, s.shape[1]), F32))
    _, rank, vals = lax.fori_loop(0, PEER_TOPK, rnd, init)
    return vals, rank


def _merge16(v1, v2, iota16):
    m0 = v1[0:1] + v2[0:1]

    def rnd(_, carry):
        n, front, z = carry
        m = jnp.max(front, axis=0, keepdims=True)
        a_sel = jnp.min(jnp.where(front == m, iota16, float(PEER_TOPK)), axis=0, keepdims=True)
        sel = iota16 == a_sel
        n = jnp.where(sel, n + 1.0, n)
        k = jnp.max(jnp.where(sel, n, 0.0), axis=0, keepdims=True)
        v2k = jnp.max(jnp.where(iota16 == k, v2, NEG_INF), axis=0, keepdims=True)
        front = jnp.where(sel, v1 + v2k, front)
        return n, front, z + jnp.exp(m - m0)

    init = (jnp.zeros_like(v1), v1 + v2[0:1], jnp.zeros_like(m0))
    n, _, z = lax.fori_loop(0, PEER_TOPK, rnd, init)
    return n, z


def _batcher_network(n):
    def merge(lo, hi, r):
        step = r * 2
        if step < hi - lo:
            yield from merge(lo, hi, step)
            yield from merge(lo + r, hi, step)
            yield from ((i, i + r) for i in range(lo + r, hi - r, step))
        else:
            yield (lo, lo + r)

    def sort(lo, hi):
        if hi - lo >= 1:
            mid = lo + (hi - lo) // 2
            yield from sort(lo, mid)
            yield from sort(mid + 1, hi)
            yield from merge(lo, hi, 1)

    return tuple(sort(0, n - 1))


_SORT16 = _batcher_network(PEER_TOPK)


def _cmpx(x, i, j):
    if x[j] is None:
        return
    if x[i] is None:
        x[i], x[j] = x[j], None
        return
    x[i], x[j] = jnp.maximum(x[i], x[j]), jnp.minimum(x[i], x[j])


def _top16_of_sublane_lists(x):
    n = PEER_TOPK
    for shift in (4, 2, 1):
        y = [None if v is None else pltpu.roll(v, shift, 0) for v in x]
        t = []
        for r in range(n):
            a, b = x[r], y[n - 1 - r]
            t.append(b if a is None else a if b is None else jnp.maximum(a, b))
        for d in (8, 4, 2, 1):
            for r in range(n):
                if r & d == 0:
                    _cmpx(t, r, r | d)
        x = t
    return x


def _sorted_top16(s):
    x = [s[v * SUBLANES:(v + 1) * SUBLANES] for v in range(s.shape[0] // SUBLANES)]
    for i, j in _SORT16:
        _cmpx(x, i, j)
    return _top16_of_sublane_lists(x)


def _count_ge(tiles, bound):
    cnt = jnp.where(tiles[0] >= bound, 1.0, 0.0)
    for v in tiles[1:]:
        cnt = cnt + jnp.where(v >= bound, 1.0, 0.0)
    return jnp.sum(cnt, axis=0, keepdims=True)


def _route_by_value(s1, s2):
    v1 = _sorted_top16(s1)
    v2 = _sorted_top16(s2)
    sub = lax.broadcasted_iota(jnp.int32, (SUBLANES, LANES), 0)

    def by_sublane(vs):
        out = vs[0]
        for r in range(1, SUBLANES):
            out = jnp.where(sub == r, vs[r], out)
        return out

    w1lo, w1hi = by_sublane(v1[:SUBLANES]), by_sublane(v1[SUBLANES:])
    w2lo, w2hi = by_sublane(v2[:SUBLANES]), by_sublane(v2[SUBLANES:])
    cands = [v1[0] + w2lo, v1[0] + w2hi, v1[1] + w2lo,
             jnp.where(sub >= 2, w1lo + v2[0], NEG_INF), w1hi + v2[0],
             jnp.where(sub >= 2, w1lo + v2[1], NEG_INF),
             jnp.where((sub >= 2) & (sub <= 4), v1[2] + w2lo, NEG_INF),
             jnp.where((sub >= 2) & (sub <= 3), v1[3] + w2lo, NEG_INF),
             jnp.where(sub == 2, v1[4] + w2lo, NEG_INF)]
    x = list(cands) + [None] * (PEER_TOPK - len(cands))
    for i, j in _SORT16:
        _cmpx(x, i, j)
    top = _top16_of_sublane_lists(x)
    tau = top[PEER_TOPK - 1]
    z = jnp.exp(top[1] - top[0]) + 1.0
    for r in range(2, PEER_TOPK):
        z = z + jnp.exp(top[r] - top[0])

    s1_tiles = [s1[v * SUBLANES:(v + 1) * SUBLANES] for v in range(N_KEYS // SUBLANES)]
    s2_tiles = [s2[v * SUBLANES:(v + 1) * SUBLANES] for v in range(N_KEYS // SUBLANES)]
    k = float(PEER_TOPK)
    tied = ((_count_ge(s1_tiles, v1[PEER_TOPK - 1]) != k) | (_count_ge(s2_tiles, v2[PEER_TOPK - 1]) != k)
            | (_count_ge(cands, tau) != k))

    n = []
    for a in range(PEER_TOPK):
        na = jnp.where(v1[a] + v2[0] >= tau, 1.0, 0.0)
        for b in range(1, PEER_TOPK // (a + 1)):
            na = na + jnp.where(v1[a] + v2[b] >= tau, 1.0, 0.0)
        n.append(na)

    def pick(vals, bits):
        if len(vals) == 1:
            return vals[0]
        mid = len(vals) // 2
        return jnp.where(bits[0], pick(vals[mid:], bits[1:]), pick(vals[:mid], bits[1:]))

    def rank_bits(x, v):
        b8 = v[7] > x
        b4 = pick([v[3], v[11]], [b8]) > x
        b2 = pick([v[1], v[5], v[9], v[13]], [b8, b4]) > x
        b1 = pick(v[0:PEER_TOPK:2], [b8, b4, b2]) > x
        return [b8, b4, b2, b1], v[PEER_TOPK - 1] > x

    key2_tiles, thr_tiles = [], []
    for x in s2_tiles:
        bits, beyond = rank_bits(x, v2)
        rank = (jnp.where(bits[0], 8.0, 0.0) + jnp.where(bits[1], 4.0, 0.0)
                + jnp.where(bits[2], 2.0, 0.0) + jnp.where(bits[3], 1.0, 0.0))
        key2_tiles.append(jnp.where(beyond, 0.0, k - rank))
    for x in s1_tiles:
        bits, beyond = rank_bits(x, v1)
        thr_tiles.append((k + 1.0) - jnp.where(beyond, 0.0, pick(n, bits)))
    key2 = jnp.concatenate(key2_tiles, axis=0)
    thr = jnp.concatenate(thr_tiles, axis=0)
    p1 =jnp.exp(s1 - v1[0][0:1]) * (1.0 / z[0:1])
    p2 = jnp.exp(s2 - v2[0][0:1])
    return key2, thr, p1, p2, jnp.where(tied, 1.0, 0.0)


def _query_kernel(x1_ref, g_ref, wq_ref, k1_ref, k2_ref,
                  xn_ref, key2_ref, thr_ref, p1_ref, p2_ref, s1_ref, s2_ref, tied_ref):
    tile = x1_ref.shape[0]
    lane_tiles = tile // LANES
    half = k1_ref.shape[1]
    xn = _rms(x1_ref[...], g_ref[...]).astype(BF16)
    xn_ref[...] = xn
    q = jnp.dot(xn, wq_ref[...], preferred_element_type=F32).astype(BF16)
    nt = (((1,), (1,)), ((), ()))
    for h in range(PEER_HEADS):
        q1 = q[:, 2 * h * half:(2 * h + 1) * half]
        q2 = q[:, (2 * h + 1) * half:(2 * h + 2) * half]
        s1 = lax.dot_general(k1_ref[...], q1, nt, preferred_element_type=F32)
        s2 = lax.dot_general(k2_ref[...], q2, nt, preferred_element_type=F32)
        for c in range(lane_tiles):
            s1_ref[h, c] = s1[:, c * LANES:(c + 1) * LANES]
            s2_ref[h, c] = s2[:, c * LANES:(c + 1) * LANES]

    tied_ref[...] = jnp.zeros_like(tied_ref)

    def route_by_value(idx, carry):
        h = idx // lane_tiles
        c = idx % lane_tiles
        key2, thr, p1, p2, tied = _route_by_value(s1_ref[h, c], s2_ref[h, c])
        key2_ref[h, c] = key2
        thr_ref[h, c] = thr
        p1_ref[h, c] = p1
        p2_ref[h, c] = p2
        tied_ref[...] = jnp.maximum(tied_ref[...], tied)
        return carry

    lax.fori_loop(0, PEER_HEADS * lane_tiles, route_by_value, 0)

    def route_by_rank(idx, carry):
        h = idx // lane_tiles
        c = idx % lane_tiles
        iota = lax.broadcasted_iota(jnp.int32, (N_KEYS, LANES), 0).astype(F32)
        s1 = s1_ref[h, c]
        s2 = s2_ref[h, c]
        v1, rank1 = _top16(s1, iota)
        v2, rank2 = _top16(s2, iota)
        n, z = _merge16(v1, v2, iota[:PEER_TOPK])
        cnt = jnp.zeros_like(rank1)
        for a in range(PEER_TOPK):
            cnt = jnp.where(rank1 == float(a), n[a:a + 1], cnt)
        key2_ref[h, c] = float(PEER_TOPK) - rank2
        thr_ref[h, c] = float(PEER_TOPK + 1) - cnt
        p1_ref[h, c] = jnp.exp(s1 - v1[0:1]) / z
        p2_ref[h, c] = jnp.exp(s2 - v2[0:1])
        return carry

    @pl.when(jnp.max(tied_ref[...]) > 0.0)
    def _():
        lax.fori_loop(0, PEER_HEADS * lane_tiles, route_by_rank, 0)


def _query(x1, g, w_q, keys1, keys2):
    t, d = x1.shape
    tm = TOKEN_TILE
    lane_tiles = tm // LANES
    tab_block = (PEER_HEADS, lane_tiles, N_KEYS, LANES)
    tab_spec = pl.BlockSpec(tab_block, lambda i: (0, i, 0, 0))
    tab_shape = jax.ShapeDtypeStruct((PEER_HEADS, t // LANES, N_KEYS, LANES), F32)
    return pl.pallas_call(
        _query_kernel,
        grid=(t // tm,),
        in_specs=[pl.BlockSpec((tm, d), lambda i: (i, 0)), _const_spec(g.shape),
                  _const_spec(w_q.shape), _const_spec(keys1.shape), _const_spec(keys2.shape)],
        out_specs=[pl.BlockSpec((tm, d), lambda i: (i, 0)), tab_spec, tab_spec, tab_spec, tab_spec],
        out_shape=[jax.ShapeDtypeStruct((t, d), BF16), tab_shape, tab_shape, tab_shape, tab_shape],
        scratch_shapes=[pltpu.VMEM(tab_block, F32), pltpu.VMEM(tab_block, F32),
                        pltpu.VMEM((1, LANES), F32)],
        compiler_params=pltpu.CompilerParams(dimension_semantics=("arbitrary",),
                                             vmem_limit_bytes=VMEM_LIMIT),
        name="query",
    )(x1, g, w_q, keys1, keys2)


def _peer_kernel(xn_ref, u_ref, vt_ref, key2_ref, thr_ref, p1_ref, p2_ref, x1_ref, g_ref,
                 yp_ref, ys_ref, acc_ref, act0_ref, act1_ref, coef0_ref, coef1_ref, *, tiles_p):
    s = pl.program_id(1)
    last = pl.num_programs(1) - 1
    eb = u_ref.shape[0] // 2
    tt = xn_ref.shape[0]
    half = tt // 2
    rows = eb // N_KEYS
    nt = (((1,), (1,)), ((), ()))
    act_refs = (act0_ref, act1_ref)
    coef_refs = (coef0_ref, coef1_ref)

    def tail_zero(r):
        return jnp.minimum(jnp.abs(r[r.shape[0] - JROWS:, r.shape[1] - LANES:]), 0.0)

    def stage_a(slot, n):
        r = lax.dot_general(u_ref[slot * eb:(slot + 1) * eb, :], xn_ref[n * half:(n + 1) * half, :], nt,
                            preferred_element_type=F32)
        act_refs[slot][:, n * half:(n + 1) * half] = r
        return tail_zero(r)

    def stage_b(slot, k, c, zero):
        act_ref, coef_ref = act_refs[slot], coef_refs[slot]
        zero_bf = zero.astype(BF16)
        for j0 in range(0, N_KEYS, JROWS):
            ws = [zero_bf] * rows
            for h in range(PEER_HEADS):
                k2 = key2_ref[h, c, j0:j0 + JROWS, :].astype(BF16)
                p2 = p2_ref[h, c, j0:j0 + JROWS, :].astype(BF16)
                for il in range(rows):
                    i = k * rows + il
                    thr = jnp.broadcast_to(thr_ref[h, c, pl.ds(i, 1), :], (JROWS, LANES)).astype(BF16)
                    p1 = jnp.broadcast_to(p1_ref[h, c, pl.ds(i, 1), :], (JROWS, LANES)).astype(BF16)
                    ws[il] = ws[il] + jnp.where(k2 >= thr, p2, jnp.zeros_like(p2)) * p1
            for il in range(rows):
                blk = (slice(il * N_KEYS + j0, il * N_KEYS + j0 + JROWS), slice(c * LANES, (c + 1) * LANES))
                coef_ref[blk] = ws[il] * _gelu(act_ref[blk].astype(BF16))

    def stage_c(slot, n):
        cols = slice(n * half, (n + 1) * half)
        r = acc_ref[:, cols] + jnp.dot(vt_ref[:, slot * eb:(slot + 1) * eb], coef_refs[slot][:, cols],
                                       preferred_element_type=F32)
        acc_ref[:, cols] = r
        return tail_zero(r)

    zero0 = jnp.zeros((JROWS, LANES), F32)

    @pl.when(s == 0)
    def _():
        acc_ref[...] = jnp.zeros_like(acc_ref)
        z = stage_a(0, 0)
        stage_a(0, 1)
        stage_b(0, 0, 0, z)
        z = stage_a(1, 0)
        stage_b(0, 0, 1, z)
        z = stage_a(1, 1)
        stage_b(0, 0, 2, z)
        stage_b(0, 0, 3, z)

    @pl.when(jnp.logical_and(s > 0, s < last))
    def _():
        ka, kb = 2 * s - 1, 2 * s
        z = stage_a(0, 0)
        stage_b(1, ka, 0, zero0)
        z2 = stage_a(0, 1)
        stage_b(1, ka, 1, z)
        z = stage_c(0, 0)
        stage_b(1, ka, 2, z2)
        z2 = stage_c(0, 1)
        stage_b(1, ka, 3, z)
        z = stage_a(1, 0)
        stage_b(0, kb, 0, z2)
        z2 = stage_a(1, 1)
        stage_b(0, kb, 1, z)
        z = stage_c(1, 0)
        stage_b(0, kb, 2, z2)
        stage_c(1, 1)
        stage_b(0, kb, 3, z)

    @pl.when(s == last)
    def _():
        ka = 2 * s - 1
        z = stage_c(0, 0)
        stage_b(1, ka, 0, zero0)
        stage_b(1, ka, 1, z)
        z = stage_c(0, 1)
        stage_b(1, ka, 2, z)
        stage_b(1, ka, 3, z)
        stage_c(1, 0)
        stage_c(1, 1)
        y = _rms(x1_ref[...] + acc_ref[...].T, g_ref[...])
        tile = pl.program_id(0)

        @pl.when(tile < tiles_p)
        def _():
            yp_ref[...] = y

        @pl.when(tile >= tiles_p)
        def _():
            ys_ref[...] = y


def _peer(xn, u_tab, vt_tab, key2, thr, p1, p2, x1, g, t_p):
    t, d = xn.shape
    n_exp = u_tab.shape[0]
    tt, eb = TOKEN_TILE, EXPERT_BLOCK
    lane_tiles = tt // LANES
    n_pairs = n_exp // (2 * eb)
    tab_spec = pl.BlockSpec((PEER_HEADS, lane_tiles, N_KEYS, LANES), lambda i, s: (0, i, 0, 0))
    tiles_p = t_p // tt
    return pl.pallas_call(
        functools.partial(_peer_kernel, tiles_p=tiles_p),
        grid=(t // tt, n_pairs + 1),
        in_specs=[pl.BlockSpec((tt, d), lambda i, s: (i, 0)),
                  pl.BlockSpec((2 * eb, d), lambda i, s: (jnp.minimum(s, n_pairs - 1), 0)),
                  pl.BlockSpec((d, 2 * eb), lambda i, s: (0, jnp.maximum(s - 1, 0))),
                  tab_spec, tab_spec, tab_spec, tab_spec,
                  pl.BlockSpec((tt, d), lambda i, s: (i, 0), pipeline_mode=pl.Buffered(1)),
                  pl.BlockSpec((1, d), lambda i, s: (0, 0))],
        out_specs=_stream_specs(tt, d, tiles_p, pipeline_mode=pl.Buffered(1)),
        out_shape=[jax.ShapeDtypeStruct((t_p, d), F32), jax.ShapeDtypeStruct((t - t_p, d), F32)],
        scratch_shapes=[pltpu.VMEM((d, tt), F32), pltpu.VMEM((eb, tt), F32), pltpu.VMEM((eb, tt), F32),
                        pltpu.VMEM((eb, tt), BF16), pltpu.VMEM((eb, tt), BF16)],
        compiler_params=pltpu.CompilerParams(dimension_semantics=("arbitrary", "arbitrary"),
                                             vmem_limit_bytes=VMEM_LIMIT),
        name="peer",
    )(xn, u_tab, vt_tab, key2, thr, p1, p2, x1, g)


def _ssm_params(lam_re, lam_im, log_step, b_re, b_im, c_re, c_im, d_skip):
    g, p = lam_re.shape
    hh = b_re.shape[2]
    step = jnp.exp(log_step)[:, None]
    decay = jnp.exp(lam_re * step)
    ab_re = decay * jnp.cos(lam_im * step)
    ab_im = decay * jnp.sin(lam_im * step)
    inv = 1.0 / (lam_re * lam_re + lam_im * lam_im)
    k_re = ((ab_re - 1.0) * lam_re + ab_im * lam_im) * inv
    k_im = (ab_im * lam_re - (ab_re - 1.0) * lam_im) * inv
    bb_re = k_re[..., None] * b_re - k_im[..., None] * b_im
    bb_im = k_re[..., None] * b_im + k_im[..., None] * b_re
    eye = jnp.eye(g, dtype=F32)

    def in_mat(bb):
        return jnp.einsum("gph,gk->ghkp", bb, eye).reshape(g * hh, g * p)

    def out_mat(cc):
        return jnp.einsum("ghp,gk->gpkh", cc, eye).reshape(g * p, g * hh)

    wb = jnp.concatenate([in_mat(bb_re), in_mat(bb_im)], axis=1).astype(BF16)
    wc = jnp.concatenate([out_mat(c_re), -out_mat(c_im)], axis=0).astype(BF16)

    a_re = ab_re.reshape(1, g * p)
    a_im = ab_im.reshape(1, g * p)
    pows = [(a_re, a_im)]
    for _ in range(SUBLANES - 1):
        q_re, q_im = pows[-1]
        pows.append((q_re * a_re - q_im * a_im, q_re * a_im + q_im * a_re))
    row = jnp.arange(SUBLANES)[:, None]
    lvl = jnp.stack([jnp.stack([jnp.where(row >= k, pows[k - 1][0], 0.0),
                                jnp.where(row >= k, pows[k - 1][1], 0.0)]) for k in (1, 2, 4)])
    pw = jnp.stack([jnp.concatenate([q[0] for q in pows], axis=0),
                    jnp.concatenate([q[1] for q in pows], axis=0)])
    return wb, wc, d_skip.reshape(1, g * hh), lvl, pw


def kernel(x_prompt, x_sample, state_ssm_re, state_ssm_im, cache_conv, norm_mix, w_in, b_gate,
           ssm_lambda_re, ssm_lambda_im, ssm_log_step, ssm_b_re, ssm_b_im, ssm_c_re, ssm_c_im,
           ssm_d, w_ssm_glu, conv_w, conv_b, conv_ln_g, conv_ln_b, w_conv_out, w_out, norm_ffn,
           peer_w_q, peer_keys1, peer_keys2, peer_u, peer_v, norm_final):
    depth = w_in.shape[0]
    assert depth == 1, "single-layer stack only"
    n_p, l_p, d = x_prompt.shape
    n_s, l_s, _ = x_sample.shape
    assert l_p % CHUNK == 0 and l_s % CHUNK == 0
    t_p, t_s = n_p * l_p, n_s * l_s
    t = t_p + t_s
    assert t_p % TOKEN_TILE == 0 and t_s % TOKEN_TILE == 0
    g, p = ssm_lambda_re.shape[1:]
    n_state = g * p
    d_ssm = g * ssm_b_re.shape[3]
    d_conv = conv_w.shape[2]
    hist_len = CONV_WIDTH - 1

    xp = x_prompt.reshape(t_p, d)
    xs = x_sample.reshape(t_s, d)

    cp, cs = l_p // CHUNK, l_s // CHUNK
    seq_p = jnp.full((n_p, cp), -1, jnp.int32).at[:, 0].set(0).reshape(-1)
    seq_s = jnp.full((n_s, cs), -1, jnp.int32).at[:, 0].set(1 + jnp.arange(n_s, dtype=jnp.int32)).reshape(-1)
    seq = jnp.concatenate([seq_p, seq_s])
    h0 = jnp.concatenate([state_ssm_re[0].reshape(n_s, n_state), state_ssm_im[0].reshape(n_s, n_state)], axis=1)
    h0 = jnp.pad(h0, ((1, 0), (0, 0)))
    hist = jnp.pad(cache_conv[0], ((1, 0), (HIST_ROWS - hist_len, 0), (0, 0)))

    wb, wc, d_skip, lvl, pw = _ssm_params(ssm_lambda_re[0], ssm_lambda_im[0], ssm_log_step[0],
                                          ssm_b_re[0], ssm_b_im[0], ssm_c_re[0], ssm_c_im[0], ssm_d[0])

    u, v, gate = _in_proj(xp, xs, norm_mix, w_in[0].astype(BF16), b_gate, d_ssm, d_conv)
    branch_a, hend = _ssm(seq, u, h0, wb, wc, d_skip, w_ssm_glu[0].astype(BF16), lvl, pw, d)
    cw = jnp.pad(conv_w[0], ((0, HIST_ROWS - CONV_WIDTH), (0, 0)))
    x1 = _mix(seq, v, hist, cw, conv_b, conv_ln_g, conv_ln_b, w_conv_out[0].astype(BF16),
              gate, branch_a, xp, xs, w_out[0].astype(BF16))
    xn2, key2, thr, p1, p2 = _query(x1, norm_ffn, peer_w_q[0].astype(BF16),
                                    peer_keys1[0].astype(BF16), peer_keys2[0].astype(BF16))
    y_p, y_s = _peer(xn2, peer_u[0].astype(BF16), peer_v[0].astype(BF16).T, key2, thr, p1, p2, x1,
                     norm_final.reshape(1, d), t_p)

    y_prompt = y_p.reshape(n_p, l_p, d)
    y_sample = y_s.reshape(n_s, l_s, d)
    last_p = hend[:n_p * cp].reshape(n_p, cp, 2 * n_state)[:, -1]
    last_s = hend[n_p * cp:].reshape(n_s, cs, 2 * n_state)[:, -1]
    v_p = jnp.stack([v[(b + 1) * l_p - hist_len:(b + 1) * l_p] for b in range(n_p)])
    v_s = v[t_p:].reshape(n_s, l_s, d_conv)[:, l_s - hist_len:]
    return (y_prompt, y_sample,
            last_p[:, :n_state].reshape(1, n_p, g, p), last_p[:, n_state:].reshape(1, n_p, g, p), v_p[None],
            last_s[:, :n_state].reshape(1, n_s, g, p), last_s[:, n_state:].reshape(1, n_s, g, p), v_s[None])
```

```python
import functools
import math

import jax
import jax.numpy as jnp
from jax import lax
from jax.experimental import pallas as pl
from jax.experimental.pallas import tpu as pltpu

F32 = jnp.float32
BF16 = jnp.bfloat16

EPS = 1e-6
CHUNK = 64
SSM_GROUP = 16
SSM_STATE = 64
CONV_WIDTH = 31
HIST_ROWS = 32
N_KEYS = 128
PEER_HEADS = 8
PEER_TOPK = 16
LANES = 128
SUBLANES = 8
TOKEN_TILE = 512
EXPERT_BLOCK = 1024
JROWS = 64
VMEM_LIMIT = 58 * 1024 * 1024
NEG_INF = float("-inf")


def _rms(x, g):
    ms = jnp.mean(x * x, axis=-1, keepdims=True)
    return (x * lax.rsqrt(ms + EPS)) * g


def _gelu(x):
    return 0.5 * x * (1.0 + lax.erf(x * (1.0 / math.sqrt(2.0))))


def _sigmoid(x):
    return 1.0 / (1.0 + jnp.exp(-x))


def _const_spec(shape):
    nd = len(shape)
    return pl.BlockSpec(shape, lambda *_: (0,) * nd)


def _stream_specs(tm, d, tiles_p, pipeline_mode=None):
    return [pl.BlockSpec((tm, d), lambda i, *_: (jnp.minimum(i, tiles_p - 1), 0), pipeline_mode=pipeline_mode),
            pl.BlockSpec((tm, d), lambda i, *_: (jnp.maximum(i - tiles_p, 0), 0), pipeline_mode=pipeline_mode)]


def _in_proj_kernel(xp_ref, xs_ref, g_ref, w_ref, bg_ref, u_ref, v_ref, gate_ref, *, d_ssm, d_conv, tiles_p):
    x = jnp.where(pl.program_id(0) < tiles_p, xp_ref[...], xs_ref[...])
    xn = _rms(x, g_ref[...]).astype(BF16)
    u_ref[...] = jnp.dot(xn, w_ref[:, :d_ssm], preferred_element_type=F32)
    pq = jnp.dot(xn, w_ref[:, d_ssm:d_ssm + 2 * d_conv], preferred_element_type=F32)
    v_ref[...] = pq[:, :d_conv] * _sigmoid(pq[:, d_conv:])
    gl = jnp.dot(xn, w_ref[:, d_ssm + 2 * d_conv:], preferred_element_type=F32)
    gate_ref[...] = _sigmoid(gl + bg_ref[...]).astype(BF16)


def _in_proj(xp, xs, g, w_in, b_gate, d_ssm, d_conv):
    d = xp.shape[1]
    t = xp.shape[0] + xs.shape[0]
    d_in = w_in.shape[1]
    n_gate = d_in - d_ssm - 2 * d_conv
    tm = TOKEN_TILE
    tiles_p = xp.shape[0] // tm
    return pl.pallas_call(
        functools.partial(_in_proj_kernel, d_ssm=d_ssm, d_conv=d_conv, tiles_p=tiles_p),
        grid=(t // tm,),
        in_specs=_stream_specs(tm, d, tiles_p) +
                 [_const_spec((1, d)), _const_spec((d, d_in)), _const_spec((1, n_gate))],
        out_specs=[pl.BlockSpec((tm, d_ssm), lambda i: (i, 0)),
                   pl.BlockSpec((tm, d_conv), lambda i: (i, 0)),
                   pl.BlockSpec((tm, n_gate), lambda i: (i, 0))],
        out_shape=[jax.ShapeDtypeStruct((t, d_ssm), F32),
                   jax.ShapeDtypeStruct((t, d_conv), F32),
                   jax.ShapeDtypeStruct((t, n_gate), BF16)],
        compiler_params=pltpu.CompilerParams(dimension_semantics=("arbitrary",),
                                             vmem_limit_bytes=VMEM_LIMIT),
        name="in_proj",
    )(xp, xs, g, w_in, b_gate)


def _ssm_kernel(seq_ref, u_ref, h0_ref, wb_ref, wc_ref, d_ref, wglu_ref, lvl_ref, pw_ref,
                a_ref, hend_ref, bu_ref, cre_ref, cim_ref, *, n_state, d_model):
    tile = u_ref.shape[0]
    chunks = tile // CHUNK
    step = pl.program_id(0)

    @pl.when(step == 0)
    def _():
        cre_ref[...] = jnp.zeros_like(cre_ref)
        cim_ref[...] = jnp.zeros_like(cim_ref)

    u = u_ref[...]
    ub = u.astype(BF16)
    dh = u.shape[1] // 2
    nh = n_state // 2
    for part in (0, n_state):
        for hf in range(2):
            cols = slice(part + hf * nh, part + (hf + 1) * nh)
            bu_ref[:, cols] = jnp.dot(ub[:, hf * dh:(hf + 1) * dh], wb_ref[hf * dh:(hf + 1) * dh, cols],
                                      preferred_element_type=F32)

    for j in range(chunks):
        seq = seq_ref[step * chunks + j]
        start = seq >= 0
        row = jnp.maximum(seq, 0)
        c_re = jnp.where(start, h0_ref[pl.ds(row, 1), :n_state], cre_ref[...])
        c_im = jnp.where(start, h0_ref[pl.ds(row, 1), n_state:], cim_ref[...])

        def block(b, carry, j=j):
            c_re, c_im = carry
            r0 = (j * (CHUNK // SUBLANES) + b) * SUBLANES
            x_re = bu_ref[pl.ds(r0, SUBLANES), :n_state]
            x_im = bu_ref[pl.ds(r0, SUBLANES), n_state:]
            for lvl, k in enumerate((1, 2, 4)):
                m_re = lvl_ref[lvl, 0]
                m_im = lvl_ref[lvl, 1]
                t_re = pltpu.roll(x_re, k, 0)
                t_im = pltpu.roll(x_im, k, 0)
                x_re, x_im = (x_re + m_re * t_re - m_im * t_im,
                              x_im + m_re * t_im + m_im * t_re)
            p_re = pw_ref[0]
            p_im = pw_ref[1]
            h_re = x_re + p_re * c_re - p_im * c_im
            h_im = x_im + p_re * c_im + p_im * c_re
            bu_ref[pl.ds(r0, SUBLANES), :n_state] = h_re
            bu_ref[pl.ds(r0, SUBLANES), n_state:] = h_im
            return h_re[SUBLANES - 1:], h_im[SUBLANES - 1:]

        for b in range(CHUNK // SUBLANES):
            c_re, c_im = block(b, (c_re, c_im))
        cre_ref[...] = c_re
        cim_ref[...] = c_im
        hend_ref[j:j + 1, :n_state] = c_re
        hend_ref[j:j + 1, n_state:] = c_im

    ys = []
    for hf in range(2):
        out = slice(hf * dh, (hf + 1) * dh)
        re = slice(hf * nh, (hf + 1) * nh)
        im = slice(n_state + hf * nh, n_state + (hf + 1) * nh)
        ys.append(jnp.dot(bu_ref[:, re].astype(BF16), wc_ref[re, out], preferred_element_type=F32)
                  + jnp.dot(bu_ref[:, im].astype(BF16), wc_ref[im, out], preferred_element_type=F32))
    y = jnp.concatenate(ys, axis=1) + d_ref[...] * u
    pq =jnp.dot(_gelu(y).astype(BF16), wglu_ref[...], preferred_element_type=F32)
    a_ref[...] = (pq[:, :d_model] * _sigmoid(pq[:, d_model:])).astype(BF16)


def _ssm(seq, u, h0, wb, wc, d_skip, wglu, lvl, pw, d_model):
    t, d_ssm = u.shape
    n_state = wb.shape[1] // 2
    tm = TOKEN_TILE
    chunks = tm // CHUNK
    grid_spec = pltpu.PrefetchScalarGridSpec(
        num_scalar_prefetch=1,
        grid=(t // tm,),
        in_specs=[pl.BlockSpec((tm, d_ssm), lambda i, f: (i, 0)),
                  _const_spec(h0.shape),
                  _const_spec(wb.shape), _const_spec(wc.shape), _const_spec(d_skip.shape),
                  _const_spec(wglu.shape), _const_spec(lvl.shape), _const_spec(pw.shape)],
        out_specs=[pl.BlockSpec((tm, d_model), lambda i, f: (i, 0)),
                   pl.BlockSpec((chunks, 2 * n_state), lambda i, f: (i, 0))],
        scratch_shapes=[pltpu.VMEM((tm, 2 * n_state), F32),
                        pltpu.VMEM((1, n_state), F32), pltpu.VMEM((1, n_state), F32)],
    )
    return pl.pallas_call(
        functools.partial(_ssm_kernel, n_state=n_state, d_model=d_model),
        grid_spec=grid_spec,
        out_shape=[jax.ShapeDtypeStruct((t, d_model), BF16),
                   jax.ShapeDtypeStruct((t // CHUNK, 2 * n_state), F32)],
        compiler_params=pltpu.CompilerParams(dimension_semantics=("arbitrary",),
                                             vmem_limit_bytes=VMEM_LIMIT),
        name="ssm",
    )(seq, u, h0, wb, wc, d_skip, wglu, lvl, pw)


def _mix_kernel(seq_ref, v_ref, hist_ref, cw_ref, cb_ref, lg_ref, lb_ref, wco_ref,
                gate_ref, a_ref, xp_ref, xs_ref, wout_ref, x1_ref, win_ref, shift_ref, conv_ref,
                *, d_model, tiles_p):
    tile = v_ref.shape[0]
    chunks = tile // CHUNK
    step = pl.program_id(0)
    lead = HIST_ROWS - (CONV_WIDTH - 1)

    @pl.when(step == 0)
    def _():
        win_ref[...] = jnp.zeros_like(win_ref)

    def chunk_body(j, carry):
        r0 = pl.multiple_of(j * CHUNK, CHUNK)
        seq = seq_ref[step * chunks + j]
        win_ref[:HIST_ROWS] = jnp.where(seq >= 0, hist_ref[jnp.maximum(seq, 0)], win_ref[CHUNK:])
        win_ref[HIST_ROWS:] = v_ref[pl.ds(r0, CHUNK), :]
        span = HIST_ROWS + CHUNK - SUBLANES
        for r in range(1, SUBLANES):
            shift_ref[r - 1] = win_ref[r:r + span, :]
        acc = None
        for k in range(CONV_WIDTH):
            q, r = divmod(lead + k, SUBLANES)
            rows = win_ref[q * SUBLANES:q * SUBLANES + CHUNK, :] if r == 0 else \
                shift_ref[r - 1, q * SUBLANES:q * SUBLANES + CHUNK, :]
            term = cw_ref[k:k + 1, :] * rows
            acc = term if acc is None else acc + term
        conv_ref[pl.ds(r0, CHUNK), :] = acc + cb_ref[...]
        return carry

    lax.fori_loop(0, chunks, chunk_body, 0)
    conv = conv_ref[...]
    mu = jnp.mean(conv, axis=-1, keepdims=True)
    xc = conv - mu
    ln = (xc * lax.rsqrt(jnp.mean(xc * xc, axis=-1, keepdims=True) + EPS)) * lg_ref[...] + lb_ref[...]
    branch_b = jnp.dot((ln * _sigmoid(ln)).astype(BF16), wco_ref[...], preferred_element_type=F32)
    ga = gate_ref[:, :d_model].astype(F32)
    gb = gate_ref[:, d_model:].astype(F32)
    m = ga * a_ref[...].astype(F32) + gb * branch_b
    x = jnp.where(step < tiles_p, xp_ref[...], xs_ref[...])
    x1_ref[...] = x + jnp.dot(m.astype(BF16), wout_ref[...], preferred_element_type=F32)


def _mix(seq, v, hist, conv_w, conv_b, ln_g, ln_b, w_conv_out, gate, branch_a, xp, xs, w_out):
    t, d_conv = v.shape
    d_model = xp.shape[1]
    tm = TOKEN_TILE
    tiles_p = xp.shape[0] // tm
    grid_spec = pltpu.PrefetchScalarGridSpec(
        num_scalar_prefetch=1,
        grid=(t // tm,),
        in_specs=[pl.BlockSpec((tm, d_conv), lambda i, f: (i, 0)),
                  _const_spec(hist.shape),
                  _const_spec(conv_w.shape), _const_spec(conv_b.shape), _const_spec(ln_g.shape),
                  _const_spec(ln_b.shape), _const_spec(w_conv_out.shape),
                  pl.BlockSpec((tm, 2 * d_model), lambda i, f: (i, 0)),
                  pl.BlockSpec((tm, d_model), lambda i, f: (i, 0))] +
                 _stream_specs(tm, d_model, tiles_p) + [_const_spec(w_out.shape)],
        out_specs=pl.BlockSpec((tm, d_model), lambda i, f: (i, 0)),
        scratch_shapes=[pltpu.VMEM((HIST_ROWS + CHUNK, d_conv), F32),
                        pltpu.VMEM((SUBLANES - 1, HIST_ROWS + CHUNK - SUBLANES, d_conv), F32),
                        pltpu.VMEM((tm, d_conv), F32)],
    )
    return pl.pallas_call(
        functools.partial(_mix_kernel, d_model=d_model, tiles_p=tiles_p),
        grid_spec=grid_spec,
        out_shape=jax.ShapeDtypeStruct((t, d_model), F32),
        compiler_params=pltpu.CompilerParams(dimension_semantics=("arbitrary",),
                                             vmem_limit_bytes=VMEM_LIMIT),
        name="mix",
    )(seq, v, hist, conv_w, conv_b, ln_g, ln_b, w_conv_out, gate, branch_a, xp, xs, w_out)


def _top16(s, iota):
    iota16 = iota[:PEER_TOPK]

    def rnd(r, carry):
        work, rank, vals = carry
        rf = jnp.asarray(r, F32)
        m = jnp.max(work, axis=0, keepdims=True)
        first = jnp.min(jnp.where(work == m, iota, float(N_KEYS)), axis=0, keepdims=True)
        sel = iota == first
        rank = jnp.where(sel, rf, rank)
        work = jnp.where(sel, NEG_INF, work)
        vals = jnp.where(iota16 == rf, m, vals)
        return work, rank, vals

    init = (s, jnp.full(s.shape, float(PEER_TOPK), F32), jnp.zeros((PEER_TOPK, s.shape[1]), F32))
    _, rank, vals = lax.fori_loop(0, PEER_TOPK, rnd, init)
    return vals, rank


def _merge16(v1, v2, iota16):
    m0 = v1[0:1] + v2[0:1]

    def rnd(_, carry):
        n, front, z = carry
        m = jnp.max(front, axis=0, keepdims=True)
        a_sel = jnp.min(jnp.where(front == m, iota16, float(PEER_TOPK)), axis=0, keepdims=True)
        sel = iota16 == a_sel
        n = jnp.where(sel, n + 1.0, n)
        k = jnp.max(jnp.where(sel, n, 0.0), axis=0, keepdims=True)
        v2k = jnp.max(jnp.where(iota16 == k, v2, NEG_INF), axis=0, keepdims=True)
        front = jnp.where(sel, v1 + v2k, front)
        return n, front, z + jnp.exp(m - m0)

    init = (jnp.zeros_like(v1), v1 + v2[0:1], jnp.zeros_like(m0))
    n, _, z = lax.fori_loop(0, PEER_TOPK, rnd, init)
    return n, z


def _batcher_network(n):
    def merge(lo, hi, r):
        step = r * 2
        if step < hi - lo:
            yield from merge(lo, hi, step)
            yield from merge(lo + r, hi, step)
            yield from ((i, i + r) for i in range(lo + r, hi - r, step))
        else:
            yield (lo, lo + r)

    def sort(lo, hi):
        if hi - lo >= 1:
            mid = lo + (hi - lo) // 2
            yield from sort(lo, mid)
            yield from sort(mid + 1, hi)
            yield from merge(lo, hi, 1)

    return tuple(sort(0, n - 1))


_SORT16 = _batcher_network(PEER_TOPK)


def _cmpx(x, i, j):
    if x[j] is None:
        return
    if x[i] is None:
        x[i], x[j] = x[j], None
        return
    x[i], x[j] = jnp.maximum(x[i], x[j]), jnp.minimum(x[i], x[j])


def _top16_of_sublane_lists(x):
    n = PEER_TOPK
    for shift in (4, 2, 1):
        y = [None if v is None else pltpu.roll(v, shift, 0) for v in x]
        t = []
        for r in range(n):
            a, b = x[r], y[n - 1 - r]
            t.append(b if a is None else a if b is None else jnp.maximum(a, b))
        for d in (8, 4, 2, 1):
            for r in range(n):
                if r & d == 0:
                    _cmpx(t, r, r | d)
        x = t
    return x


def _sorted_top16(s):
    x = [s[v * SUBLANES:(v + 1) * SUBLANES] for v in range(s.shape[0] // SUBLANES)]
    for i, j in _SORT16:
        _cmpx(x, i, j)
    return _top16_of_sublane_lists(x)


def _count_ge(tiles, bound):
    cnt = jnp.where(tiles[0] >= bound, 1.0, 0.0)
    for v in tiles[1:]:
        cnt = cnt + jnp.where(v >= bound, 1.0, 0.0)
    return jnp.sum(cnt, axis=0, keepdims=True)


def _route_by_value(s1, s2):
    v1 = _sorted_top16(s1)
    v2 = _sorted_top16(s2)
    sub = lax.broadcasted_iota(jnp.int32, (SUBLANES, LANES), 0)

    def by_sublane(vs):
        out = vs[0]
        for r in range(1, SUBLANES):
            out = jnp.where(sub == r, vs[r], out)
        return out

    w1lo, w1hi = by_sublane(v1[:SUBLANES]), by_sublane(v1[SUBLANES:])
    w2lo, w2hi = by_sublane(v2[:SUBLANES]), by_sublane(v2[SUBLANES:])
    cands = [v1[0] + w2lo, v1[0] + w2hi, v1[1] + w2lo,
             jnp.where(sub >= 2, w1lo + v2[0], NEG_INF), w1hi + v2[0],
             jnp.where(sub >= 2, w1lo + v2[1], NEG_INF),
             jnp.where((sub >= 2) & (sub <= 4), v1[2] + w2lo, NEG_INF),
             jnp.where((sub >= 2) & (sub <= 3), v1[3] + w2lo, NEG_INF),
             jnp.where(sub == 2, v1[4] + w2lo, NEG_INF)]
    x = list(cands) + [None] * (PEER_TOPK - len(cands))
    for i, j in _SORT16:
        _cmpx(x, i, j)
    top = _top16_of_sublane_lists(x)
    tau = top[PEER_TOPK - 1]
    z = jnp.exp(top[1] - top[0]) + 1.0
    for r in range(2, PEER_TOPK):
        z = z + jnp.exp(top[r] - top[0])

    s1_tiles = [s1[v * SUBLANES:(v + 1) * SUBLANES] for v in range(N_KEYS // SUBLANES)]
    s2_tiles = [s2[v * SUBLANES:(v + 1) * SUBLANES] for v in range(N_KEYS // SUBLANES)]
    k = float(PEER_TOPK)
    tied = ((_count_ge(s1_tiles, v1[PEER_TOPK - 1]) != k) | (_count_ge(s2_tiles, v2[PEER_TOPK - 1]) != k)
            | (_count_ge(cands, tau) != k))

    n = []
    for a in range(PEER_TOPK):
        na = jnp.where(v1[a] + v2[0] >= tau, 1.0, 0.0)
        for b in range(1, PEER_TOPK // (a + 1)):
            na = na + jnp.where(v1[a] + v2[b] >= tau, 1.0, 0.0)
        n.append(na)

    def pick(vals, bits):
        if len(vals) == 1:
            return vals[0]
        mid = len(vals) // 2
        return jnp.where(bits[0], pick(vals[mid:], bits[1:]), pick(vals[:mid], bits[1:]))

    def rank_bits(x, v):
        b8 = v[7] > x
        b4 = pick([v[3], v[11]], [b8]) > x
        b2 = pick([v[1], v[5], v[9], v[13]], [b8, b4]) > x
        b1 = pick(v[0:PEER_TOPK:2], [b8, b4, b2]) > x
        return [b8, b4, b2, b1], v[PEER_TOPK - 1] > x

    key2_tiles, thr_tiles = [], []
    for x in s2_tiles:
        bits, beyond = rank_bits(x, v2)
        rank = (jnp.where(bits[0], 8.0, 0.0) + jnp.where(bits[1], 4.0, 0.0)
                + jnp.where(bits[2], 2.0, 0.0) + jnp.where(bits[3], 1.0, 0.0))
        key2_tiles.append(jnp.where(beyond, 0.0, k - rank))
    for x in s1_tiles:
        bits, beyond = rank_bits(x, v1)
        thr_tiles.append((k + 1.0) - jnp.where(beyond, 0.0, pick(n, bits)))
    key2 = jnp.concatenate(key2_tiles, axis=0)
    thr = jnp.concatenate(thr_tiles, axis=0)
    p1 =jnp.exp(s1 - v1[0][0:1]) * (1.0 / z[0:1])
    p2 = jnp.exp(s2 - v2[0][0:1])
    return key2, thr, p1, p2, jnp.where(tied, 1.0, 0.0)


def _query_kernel(x1_ref, g_ref, wq_ref, k1_ref, k2_ref,
                  xn_ref, key2_ref, thr_ref, p1_ref, p2_ref, s1_ref, s2_ref, tied_ref):
    tile = x1_ref.shape[0]
    lane_tiles = tile // LANES
    half = k1_ref.shape[1]
    xn = _rms(x1_ref[...], g_ref[...]).astype(BF16)
    xn_ref[...] = xn
    q = jnp.dot(xn, wq_ref[...], preferred_element_type=F32).astype(BF16)
    nt = (((1,), (1,)), ((), ()))
    for h in range(PEER_HEADS):
        q1 = q[:, 2 * h * half:(2 * h + 1) * half]
        q2 = q[:, (2 * h + 1) * half:(2 * h + 2) * half]
        s1 = lax.dot_general(k1_ref[...], q1, nt, preferred_element_type=F32)
        s2 = lax.dot_general(k2_ref[...], q2, nt, preferred_element_type=F32)
        for c in range(lane_tiles):
            s1_ref[h, c] = s1[:, c * LANES:(c + 1) * LANES]
            s2_ref[h, c] = s2[:, c * LANES:(c + 1) * LANES]

    tied_ref[...] = jnp.zeros_like(tied_ref)

    def route_by_value(idx, carry):
        h = idx // lane_tiles
        c = idx % lane_tiles
        key2, thr, p1, p2, tied = _route_by_value(s1_ref[h, c], s2_ref[h, c])
        key2_ref[h, c] = key2
        thr_ref[h, c] = thr
        p1_ref[h, c] = p1
        p2_ref[h, c] = p2
        tied_ref[...] = jnp.maximum(tied_ref[...], tied)
        return carry

    lax.fori_loop(0, PEER_HEADS * lane_tiles, route_by_value, 0)

    def route_by_rank(idx, carry):
        h = idx // lane_tiles
        c = idx % lane_tiles
        iota = lax.broadcasted_iota(jnp.int32, (N_KEYS, LANES), 0).astype(F32)
        s1 = s1_ref[h, c]
        s2 = s2_ref[h, c]
        v1, rank1 = _top16(s1, iota)
        v2, rank2 = _top16(s2, iota)
        n, z = _merge16(v1, v2, iota[:PEER_TOPK])
        cnt = jnp.zeros_like(rank1)
        for a in range(PEER_TOPK):
            cnt = jnp.where(rank1 == float(a), n[a:a + 1], cnt)
        key2_ref[h, c] = float(PEER_TOPK) - rank2
        thr_ref[h, c] = float(PEER_TOPK + 1) - cnt
        p1_ref[h, c] = jnp.exp(s1 - v1[0:1]) / z
        p2_ref[h, c] = jnp.exp(s2 - v2[0:1])
        return carry

    @pl.when(jnp.max(tied_ref[...]) > 0.0)
    def _():
        lax.fori_loop(0, PEER_HEADS * lane_tiles, route_by_rank, 0)


def _query(x1, g, w_q, keys1, keys2):
    t, d = x1.shape
    tm = TOKEN_TILE
    lane_tiles = tm // LANES
    tab_block = (PEER_HEADS, lane_tiles, N_KEYS, LANES)
    tab_spec = pl.BlockSpec(tab_block, lambda i: (0, i, 0, 0))
    tab_shape = jax.ShapeDtypeStruct((PEER_HEADS, t // LANES, N_KEYS, LANES), F32)
    return pl.pallas_call(
        _query_kernel,
        grid=(t // tm,),
        in_specs=[pl.BlockSpec((tm, d), lambda i: (i, 0)), _const_spec(g.shape),
                  _const_spec(w_q.shape), _const_spec(keys1.shape), _const_spec(keys2.shape)],
        out_specs=[pl.BlockSpec((tm, d), lambda i: (i, 0)), tab_spec, tab_spec, tab_spec, tab_spec],
        out_shape=[jax.ShapeDtypeStruct((t, d), BF16), tab_shape, tab_shape, tab_shape, tab_shape],
        scratch_shapes=[pltpu.VMEM(tab_block, F32), pltpu.VMEM(tab_block, F32),
                        pltpu.VMEM((1, LANES), F32)],
        compiler_params=pltpu.CompilerParams(dimension_semantics=("arbitrary",),
                                             vmem_limit_bytes=VMEM_LIMIT),
        name="query",
    )(x1, g, w_q, keys1, keys2)


def _peer_kernel(xn_ref, u_ref, vt_ref, key2_ref, thr_ref, p1_ref, p2_ref, x1_ref, g_ref,
                 yp_ref, ys_ref, acc_ref, act0_ref, act1_ref, coef0_ref, coef1_ref, *, tiles_p):
    s = pl.program_id(1)
    last = pl.num_programs(1) - 1
    eb = u_ref.shape[0] // 2
    tt = xn_ref.shape[0]
    half = tt // 2
    rows = eb // N_KEYS
    nt = (((1,), (1,)), ((), ()))
    act_refs = (act0_ref, act1_ref)
    coef_refs = (coef0_ref, coef1_ref)

    def tail_zero(r):
        return jnp.minimum(jnp.abs(r[r.shape[0] - JROWS:, r.shape[1] - LANES:]), 0.0)

    def stage_a(slot, n):
        r = lax.dot_general(u_ref[slot * eb:(slot + 1) * eb, :], xn_ref[n * half:(n + 1) * half, :], nt,
                            preferred_element_type=F32)
        act_refs[slot][:, n * half:(n + 1) * half] = r
        return tail_zero(r)

    def stage_b(slot, k, c, zero):
        act_ref, coef_ref = act_refs[slot], coef_refs[slot]
        zero_bf = zero.astype(BF16)
        for j0 in range(0, N_KEYS, JROWS):
            ws = [zero_bf] * rows
            for h in range(PEER_HEADS):
                k2 = key2_ref[h, c, j0:j0 + JROWS, :].astype(BF16)
                p2 = p2_ref[h, c, j0:j0 + JROWS, :].astype(BF16)
                for il in range(rows):
                    i = k * rows + il
                    thr = jnp.broadcast_to(thr_ref[h, c, pl.ds(i, 1), :], (JROWS, LANES)).astype(BF16)
                    p1 = jnp.broadcast_to(p1_ref[h, c, pl.ds(i, 1), :], (JROWS, LANES)).astype(BF16)
                    ws[il] = ws[il] + jnp.where(k2 >= thr, p2, jnp.zeros_like(p2)) * p1
            for il in range(rows):
                blk = (slice(il * N_KEYS + j0, il * N_KEYS + j0 + JROWS), slice(c * LANES, (c + 1) * LANES))
                coef_ref[blk] = ws[il] * _gelu(act_ref[blk].astype(BF16))

    def stage_c(slot, n):
        cols = slice(n * half, (n + 1) * half)
        r = acc_ref[:, cols] + jnp.dot(vt_ref[:, slot * eb:(slot + 1) * eb], coef_refs[slot][:, cols],
                                       preferred_element_type=F32)
        acc_ref[:, cols] = r
        return tail_zero(r)

    zero0 = jnp.zeros((JROWS, LANES), F32)

    @pl.when(s == 0)
    def _():
        acc_ref[...] = jnp.zeros_like(acc_ref)
        z = stage_a(0, 0)
        stage_a(0, 1)
        stage_b(0, 0, 0, z)
        z = stage_a(1, 0)
        stage_b(0, 0, 1, z)
        z = stage_a(1, 1)
        stage_b(0, 0, 2, z)
        stage_b(0, 0, 3, z)

    @pl.when(jnp.logical_and(s > 0, s < last))
    def _():
        ka, kb = 2 * s - 1, 2 * s
        z = stage_a(0, 0)
        stage_b(1, ka, 0, zero0)
        z2 = stage_a(0, 1)
        stage_b(1, ka, 1, z)
        z = stage_c(0, 0)
        stage_b(1, ka, 2, z2)
        z2 = stage_c(0, 1)
        stage_b(1, ka, 3, z)
        z = stage_a(1, 0)
        stage_b(0, kb, 0, z2)
        z2 = stage_a(1, 1)
        stage_b(0, kb, 1, z)
        z = stage_c(1, 0)
        stage_b(0, kb, 2, z2)
        stage_c(1, 1)
        stage_b(0, kb, 3, z)

    @pl.when(s == last)
    def _():
        ka = 2 * s - 1
        z = stage_c(0, 0)
        stage_b(1, ka, 0, zero0)
        stage_b(1, ka, 1, z)
        z = stage_c(0, 1)
        stage_b(1, ka, 2, z)
        stage_b(1, ka, 3, z)
        stage_c(1, 0)
        stage_c(1, 1)
        y = _rms(x1_ref[...] + acc_ref[...].T, g_ref[...])
        tile = pl.program_id(0)

        @pl.when(tile < tiles_p)
        def _():
            yp_ref[...] = y

        @pl.when(tile >= tiles_p)
        def _():
            ys_ref[...] = y


def _peer(xn, u_tab, vt_tab, key2, thr, p1, p2, x1, g, t_p):
    t, d = xn.shape
    n_exp = u_tab.shape[0]
    tt, eb = TOKEN_TILE, EXPERT_BLOCK
    lane_tiles = tt // LANES
    n_pairs = n_exp // (2 * eb)
    tab_spec = pl.BlockSpec((PEER_HEADS, lane_tiles, N_KEYS, LANES), lambda i, s: (0, i, 0, 0))
    tiles_p = t_p // tt
    return pl.pallas_call(
        functools.partial(_peer_kernel, tiles_p=tiles_p),
        grid=(t // tt, n_pairs + 1),
        in_specs=[pl.BlockSpec((tt, d), lambda i, s: (i, 0)),
                  pl.BlockSpec((2 * eb, d), lambda i, s: (jnp.minimum(s, n_pairs - 1), 0)),
                  pl.BlockSpec((d, 2 * eb), lambda i, s: (0, jnp.maximum(s - 1, 0))),
                  tab_spec, tab_spec, tab_spec, tab_spec,
                  pl.BlockSpec((tt, d), lambda i, s: (i, 0), pipeline_mode=pl.Buffered(1)),
                  pl.BlockSpec((1, d), lambda i, s: (0, 0))],
        out_specs=_stream_specs(tt, d, tiles_p, pipeline_mode=pl.Buffered(1)),
        out_shape=[jax.ShapeDtypeStruct((t_p, d), F32), jax.ShapeDtypeStruct((t - t_p, d), F32)],
        scratch_shapes=[pltpu.VMEM((d, tt), F32), pltpu.VMEM((eb, tt), F32), pltpu.VMEM((eb, tt), F32),
                        pltpu.VMEM((eb, tt), BF16), pltpu.VMEM((eb, tt), BF16)],
        compiler_params=pltpu.CompilerParams(dimension_semantics=("arbitrary", "arbitrary"),
                                             vmem_limit_bytes=VMEM_LIMIT),
        name="peer",
    )(xn, u_tab, vt_tab, key2, thr, p1, p2, x1, g)


def _ssm_params(lam_re, lam_im, log_step, b_re, b_im, c_re, c_im, d_skip):
    g, p = lam_re.shape
    hh = b_re.shape[2]
    step = jnp.exp(log_step)[:, None]
    decay = jnp.exp(lam_re * step)
    ab_re = decay * jnp.cos(lam_im * step)
    ab_im = decay * jnp.sin(lam_im * step)
    inv = 1.0 / (lam_re * lam_re + lam_im * lam_im)
    k_re = ((ab_re - 1.0) * lam_re + ab_im * lam_im) * inv
    k_im = (ab_im * lam_re - (ab_re - 1.0) * lam_im) * inv
    bb_re = k_re[..., None] * b_re - k_im[..., None] * b_im
    bb_im = k_re[..., None] * b_im + k_im[..., None] * b_re
    eye = jnp.eye(g, dtype=F32)

    def in_mat(bb):
        return jnp.einsum("gph,gk->ghkp", bb, eye).reshape(g * hh, g * p)

    def out_mat(cc):
        return jnp.einsum("ghp,gk->gpkh", cc, eye).reshape(g * p, g * hh)

    wb = jnp.concatenate([in_mat(bb_re), in_mat(bb_im)], axis=1).astype(BF16)
    wc = jnp.concatenate([out_mat(c_re), -out_mat(c_im)], axis=0).astype(BF16)

    a_re = ab_re.reshape(1, g * p)
    a_im = ab_im.reshape(1, g * p)
    pows = [(a_re, a_im)]
    for _ in range(SUBLANES - 1):
        q_re, q_im = pows[-1]
        pows.append((q_re * a_re - q_im * a_im, q_re * a_im + q_im * a_re))
    row = jnp.arange(SUBLANES)[:, None]
    lvl = jnp.stack([jnp.stack([jnp.where(row >= k, pows[k - 1][0], 0.0),
                                jnp.where(row >= k, pows[k - 1][1], 0.0)]) for k in (1, 2, 4)])
    pw = jnp.stack([jnp.concatenate([q[0] for q in pows], axis=0),
                    jnp.concatenate([q[1] for q in pows], axis=0)])
    return wb, wc, d_skip.reshape(1, g * hh), lvl, pw


def kernel(x_prompt, x_sample, state_ssm_re, state_ssm_im, cache_conv, norm_mix, w_in, b_gate,
           ssm_lambda_re, ssm_lambda_im, ssm_log_step, ssm_b_re, ssm_b_im, ssm_c_re, ssm_c_im,
           ssm_d, w_ssm_glu, conv_w, conv_b, conv_ln_g, conv_ln_b, w_conv_out, w_out, norm_ffn,
           peer_w_q, peer_keys1, peer_keys2, peer_u, peer_v, norm_final):
    depth = w_in.shape[0]
    assert depth == 1, "single-layer stack only"
    n_p, l_p, d = x_prompt.shape
    n_s, l_s, _ = x_sample.shape
    assert l_p % CHUNK == 0 and l_s % CHUNK == 0
    t_p, t_s = n_p * l_p, n_s * l_s
    t = t_p + t_s
    assert t_p % TOKEN_TILE == 0 and t_s % TOKEN_TILE == 0
    g, p = ssm_lambda_re.shape[1:]
    n_state = g * p
    d_ssm = g * ssm_b_re.shape[3]
    d_conv = conv_w.shape[2]
    hist_len = CONV_WIDTH - 1

    xp = x_prompt.reshape(t_p, d)
    xs = x_sample.reshape(t_s, d)

    cp, cs = l_p // CHUNK, l_s // CHUNK
    seq_p = jnp.full((n_p, cp), -1, jnp.int32).at[:, 0].set(0).reshape(-1)
    seq_s = jnp.full((n_s, cs), -1, jnp.int32).at[:, 0].set(1 + jnp.arange(n_s, dtype=jnp.int32)).reshape(-1)
    seq = jnp.concatenate([seq_p, seq_s])
    h0 = jnp.concatenate([state_ssm_re[0].reshape(n_s, n_state), state_ssm_im[0].reshape(n_s, n_state)], axis=1)
    h0 = jnp.pad(h0, ((1, 0), (0, 0)))
    hist = jnp.pad(cache_conv[0], ((1, 0), (HIST_ROWS - hist_len, 0), (0, 0)))

    wb, wc, d_skip, lvl, pw = _ssm_params(ssm_lambda_re[0], ssm_lambda_im[0], ssm_log_step[0],
                                          ssm_b_re[0], ssm_b_im[0], ssm_c_re[0], ssm_c_im[0], ssm_d[0])

    u, v, gate = _in_proj(xp, xs, norm_mix, w_in[0].astype(BF16), b_gate, d_ssm, d_conv)
    branch_a, hend = _ssm(seq, u, h0, wb, wc, d_skip, w_ssm_glu[0].astype(BF16), lvl, pw, d)
    cw = jnp.pad(conv_w[0], ((0, HIST_ROWS - CONV_WIDTH), (0, 0)))
    x1 = _mix(seq, v, hist, cw, conv_b, conv_ln_g, conv_ln_b, w_conv_out[0].astype(BF16),
              gate, branch_a, xp, xs, w_out[0].astype(BF16))
    xn2, key2, thr, p1, p2 = _query(x1, norm_ffn, peer_w_q[0].astype(BF16),
                                    peer_keys1[0].astype(BF16), peer_keys2[0].astype(BF16))
    y_p, y_s = _peer(xn2, peer_u[0].astype(BF16), peer_v[0].astype(BF16).T, key2, thr, p1, p2, x1,
                     norm_final.reshape(1, d), t_p)

    y_prompt = y_p.reshape(n_p, l_p, d)
    y_sample = y_s.reshape(n_s, l_s, d)
    last_p = hend[:n_p * cp].reshape(n_p, cp, 2 * n_state)[:, -1]
    last_s = hend[n_p * cp:].reshape(n_s, cs, 2 * n_state)[:, -1]
    v_p = jnp.stack([v[(b + 1) * l_p - hist_len:(b + 1) * l_p] for b in range(n_p)])
    v_s = v[t_p:].reshape(n_s, l_s, d_conv)[:, l_s - hist_len:]
    return (y_prompt, y_sample,
            last_p[:, :n_state].reshape(1, n_p, g, p), last_p[:, n_state:].reshape(1, n_p, g, p), v_p[None],
            last_s[:, :n_state].reshape(1, n_s, g, p), last_s[:, n_state:].reshape(1, n_s, g, p), v_s[None])
```
